```python
import jax, jax.numpy as jnp
from jax import lax
import numpy as np

D_MODEL = 1024
BATCH = 8
SEQ = 2048
DEPTH = 2
DEC_BATCH = 32
DEC_SEQ = 4
PAST_LEN = 16384
PAGE_SIZE = 128

D_HEAD = 64
SB_HEADS = 8
SB_W = SB_HEADS * D_HEAD
RET_HEADS = 4
RET_DK = 64
RET_DV = 128
RET_QK_W = RET_HEADS * RET_DK
RET_V_W = RET_HEADS * RET_DV
RET_CHUNK = 128
DSA_HEADS = 8
DSA_KV_HEADS = 2
DSA_W = DSA_HEADS * D_HEAD
DSA_KV_W = DSA_KV_HEADS * D_HEAD
IDX_HEADS = 4
IDX_DIM = 64
DSA_TOPK_MAX = 256
Q_BLOCK = 128
N_BRANCH = 3
IN_SPLITS = (SB_W, SB_W, SB_W, RET_QK_W, RET_QK_W, RET_V_W, RET_V_W, DSA_W, DSA_KV_W, DSA_KV_W, IDX_HEADS * IDX_DIM, IDX_HEADS, IDX_DIM, N_BRANCH * D_MODEL)
IN_WIDTH = sum(IN_SPLITS)
PEER_HEADS = 8
PEER_NKEYS = 128
PEER_EXPERTS = PEER_NKEYS * PEER_NKEYS
PEER_DQ = 256
PEER_TOPK = 16
PEER_BLOCK = 128
RMS_EPS = 1e-6
GN_EPS = 1e-5
ROPE_BASE = 10000.0

kernel_name = 'sb_ret_dsa_peer_hybrid_step'


def rms_norm(x, g):
    xf = x.astype(jnp.float32)
    y = xf * lax.rsqrt(jnp.mean(xf * xf, axis=-1, keepdims=True) + RMS_EPS)
    return (y * g.astype(jnp.float32)).astype(x.dtype)


def ada_params(c, w, b):
    p = (jax.nn.silu(c) @ w + b)[:, None, :]
    return jnp.split(p, 3, axis=-1)


def modulated_norm(x, g, shift, scale):
    return rms_norm(x, g) * (1 + scale) + shift


def rotary(x, pos):
    half = x.shape[-1] // 2
    freqs = ROPE_BASE ** (-jnp.arange(half, dtype=jnp.float32) / half)
    ang = pos.astype(jnp.float32)[:, None] * freqs[None, :]
    cos = jnp.cos(ang)[:, None, :]
    sin = jnp.sin(ang)[:, None, :]
    xf = x.astype(jnp.float32)
    x1, x2 = xf[..., :half], xf[..., half:]
    return jnp.concatenate([x1 * cos - x2 * sin, x1 * sin + x2 * cos], axis=-1).astype(x.dtype)


def project_mixers(h, w_in, sb_qn_g, sb_kn_g, dsa_qn_g, dsa_kn_g, pos):
    n, t, _ = h.shape
    cuts = np.cumsum(IN_SPLITS)[:-1].tolist()
    (sb_q, sb_k, sb_v, r_q, r_k, r_v, r_g, d_q, d_k, d_v, i_q, i_w, i_k, gates) = jnp.split(h @ w_in, cuts, axis=-1)

    def heads(a, nh):
        return a.reshape(n, t, nh, -1)
    return dict(
        sb_q=rms_norm(heads(sb_q, SB_HEADS), sb_qn_g),
        sb_k=rms_norm(heads(sb_k, SB_HEADS), sb_kn_g),
        sb_v=heads(sb_v, SB_HEADS),
        r_q=rotary(heads(r_q, RET_HEADS), pos),
        r_k=rotary(heads(r_k, RET_HEADS), pos) * RET_DK ** -0.5,
        r_v=heads(r_v, RET_HEADS),
        r_g=r_g,
        d_q=rms_norm(heads(d_q, DSA_HEADS), dsa_qn_g),
        d_k=rms_norm(heads(d_k, DSA_KV_HEADS), dsa_kn_g),
        d_v=heads(d_v, DSA_KV_HEADS),
        i_q=heads(i_q, IDX_HEADS),
        i_w=i_w * IDX_HEADS ** -0.5,
        i_k=i_k,
        gates=gates)


def stick_breaking_weights(z, mask):
    zf = z.astype(jnp.float32)
    log_keep = jnp.where(mask, jax.nn.log_sigmoid(-zf), 0.0)
    after = lax.cumsum(log_keep, axis=zf.ndim - 1, reverse=True) - log_keep
    return jnp.where(mask, jnp.exp(jax.nn.log_sigmoid(zf) + after), 0.0)


def sb_prompt(q, k, v):
    b, s, h, d = q.shape
    nb = s // Q_BLOCK
    qb = q.reshape(b, nb, Q_BLOCK, h, d).swapaxes(0, 1)
    k_pos = jnp.arange(s)
    scale = d ** -0.5

    def block(args):
        qblk, i = args
        q_pos = i * Q_BLOCK + jnp.arange(Q_BLOCK)
        z = jnp.einsum('bqhd,bkhd->bhqk', qblk, k) * scale
        a = stick_breaking_weights(z, k_pos[None, :] < q_pos[:, None])
        return jnp.einsum('bhqk,bkhd->bqhd', a.astype(v.dtype), v)
    out = lax.map(block, (qb, jnp.arange(nb)))
    return out.swapaxes(0, 1).reshape(b, s, h, d)


def sb_sample(q, k_new, v_new, k_past, v_past):
    t = q.shape[1]
    p = k_past.shape[1]
    scale = q.shape[-1] ** -0.5
    z = jnp.concatenate([jnp.einsum('nthd,nphd->nhtp', q, k_past),
                         jnp.einsum('nthd,nphd->nhtp', q, k_new)], axis=-1) * scale
    q_pos = p + jnp.arange(t)
    k_pos = jnp.arange(p + t)
    a = stick_breaking_weights(z, k_pos[None, :] < q_pos[:, None]).astype(v_new.dtype)
    return (jnp.einsum('nhtp,nphd->nthd', a[..., :p], v_past)
            + jnp.einsum('nhtp,nphd->nthd', a[..., p:], v_new))


def ret_log_gamma():
    return jnp.log1p(-jnp.exp2(-5.0 - jnp.arange(RET_HEADS, dtype=jnp.float32)))


def retention_chunk(q, k, v, s0, log_gamma):
    c = q.shape[1]
    qf, kf, vf = q.astype(jnp.float32), k.astype(jnp.float32), v.astype(jnp.float32)
    idx = jnp.arange(c, dtype=jnp.float32)
    diff = idx[:, None] - idx[None, :]
    decay = jnp.where(diff[None] >= 0, jnp.exp(jnp.maximum(diff, 0.0)[None] * log_gamma[:, None, None]), 0.0)
    inner = jnp.einsum('nihk,njhk->nhij', qf, kf) * decay[None]
    o = jnp.einsum('nhij,njhv->nihv', inner, vf)
    q_decay = jnp.exp((idx + 1.0)[:, None] * log_gamma[None, :])
    o = o + jnp.einsum('nihk,nhkv->nihv', qf, s0) * q_decay[None, :, :, None]
    k_decay = jnp.exp((c - 1.0 - idx)[:, None] * log_gamma[None, :])
    s1 = (jnp.exp(c * log_gamma)[None, :, None, None] * s0
          + jnp.einsum('njhk,njhv->nhkv', kf * k_decay[None, :, :, None], vf))
    return o, s1


def retention_prompt(q, k, v, log_gamma):
    b, s, h, dk = q.shape
    nc = s // RET_CHUNK

    def to_chunks(a):
        return a.reshape(b, nc, RET_CHUNK, h, a.shape[-1]).swapaxes(0, 1)

    def step(state, xs):
        qc, kc, vc = xs
        o, state = retention_chunk(qc, kc, vc, state, log_gamma)
        return state, o
    s0 = jnp.zeros((b, h, dk, v.shape[-1]), jnp.float32)
    s_final, o = lax.scan(step, s0, (to_chunks(q), to_chunks(k), to_chunks(v)))
    return o.swapaxes(0, 1).reshape(b, s, h, -1), s_final


def retention_out(o, r_g, gn_g):
    n, t = o.shape[:2]
    mu = jnp.mean(o, axis=-1, keepdims=True)
    var = jnp.mean(jnp.square(o - mu), axis=-1, keepdims=True)
    on = ((o - mu) * lax.rsqrt(var + GN_EPS)).reshape(n, t, -1) * gn_g.astype(jnp.float32)
    return jax.nn.silu(r_g) * on.astype(r_g.dtype)


def indexer_scores(q_idx, w_idx, k_idx):
    dots = jnp.einsum('nthd,nld->nthl', q_idx, k_idx).astype(jnp.float32) * IDX_DIM ** -0.5
    return jnp.einsum('nth,nthl->ntl', w_idx.astype(jnp.float32), jax.nn.relu(dots))


def sparse_attend(q, ks, vs, valid):
    n, t, hq, d = q.shape
    hkv = ks.shape[-2]
    qg = q.reshape(n, t, hkv, hq // hkv, d)
    logits = jnp.einsum('ntgjd,ntkgd->ntgjk', qg, ks).astype(jnp.float32) * d ** -0.5
    logits = jnp.where(valid[:, :, None, None, :], logits, -jnp.inf)
    p = jax.nn.softmax(logits, axis=-1)
    out = jnp.einsum('ntgjk,ntkgd->ntgjd', p.astype(vs.dtype), vs)
    return out.reshape(n, t, hq, d)


def dsa_prompt(q, k, v, q_idx, w_idx, k_idx, topk):
    b, s, hq, d = q.shape
    nb = s // Q_BLOCK

    def blocks(a):
        return a.reshape(b, nb, Q_BLOCK, *a.shape[2:]).swapaxes(0, 1)
    k_pos = jnp.arange(s)
    gather_rows = jax.vmap(lambda rows, ids: rows[ids])

    def block(args):
        qb, qib, wb, i = args
        q_pos = i * Q_BLOCK + jnp.arange(Q_BLOCK)
        score = indexer_scores(qib, wb, k_idx)
        score = jnp.where((k_pos[None, :] <= q_pos[:, None])[None], score, -jnp.inf)
        _, sel = lax.top_k(score, topk)
        valid = sel <= q_pos[None, :, None]
        return sparse_attend(qb, gather_rows(k, sel), gather_rows(v, sel), valid)
    out = lax.map(block, (blocks(q), blocks(q_idx), blocks(w_idx), jnp.arange(nb)))
    return out.swapaxes(0, 1).reshape(b, s, hq, d)


def dsa_sample(q, k_new, v_new, q_idx, w_idx, kidx_new, kidx_past, pool_k, pool_v, layer, page_table, topk):
    n, t = q.shape[:2]
    p = kidx_past.shape[1]
    ps = pool_k.shape[2]
    score = jnp.concatenate([indexer_scores(q_idx, w_idx, kidx_past),
                             indexer_scores(q_idx, w_idx, kidx_new)], axis=-1)
    q_pos = p + jnp.arange(t)
    k_pos = jnp.arange(p + t)
    score = jnp.where((k_pos[None, :] <= q_pos[:, None])[None], score, -jnp.inf)
    _, sel = lax.top_k(score, topk)
    valid = sel <= q_pos[None, :, None]
    in_past = sel < p
    past_i = jnp.minimum(sel, p - 1)
    phys = page_table[jnp.arange(n)[:, None, None], past_i // ps]
    off = past_i % ps
    new_i = jnp.clip(sel - p, 0, t - 1)
    gather_rows = jax.vmap(lambda rows, ids: rows[ids])

    def pick(pool, new):
        return jnp.where(in_past[..., None, None], pool[layer, phys, off], gather_rows(new, new_i))
    return sparse_attend(q, pick(pool_k, k_new), pick(pool_v, v_new), valid)


def merge_branches(gates, o_sb, o_ret, o_dsa, w_br_sb, w_br_ret, w_br_dsa, w_out):
    n, t = o_sb.shape[:2]
    g = jax.nn.sigmoid(gates.astype(jnp.float32)).astype(gates.dtype).reshape(n, t, N_BRANCH, D_MODEL)
    merged = (g[:, :, 0] * (o_sb.reshape(n, t, -1) @ w_br_sb)
              + g[:, :, 1] * (o_ret @ w_br_ret)
              + g[:, :, 2] * (o_dsa.reshape(n, t, -1) @ w_br_dsa))
    return merged @ w_out


def peer_ffn(h, w_q, sub_keys, u, v):
    shp = h.shape
    x = h.reshape(-1, shp[-1])
    t = x.shape[0]
    nblk = -(-t // PEER_BLOCK)
    x = jnp.pad(x, ((0, nblk * PEER_BLOCK - t), (0, 0)))

    def block(xb):
        q = (xb @ w_q).reshape(PEER_BLOCK, PEER_HEADS, 2, PEER_DQ // 2)
        s = jnp.einsum('thpc,hpnc->thpn', q, sub_keys).astype(jnp.float32)
        s1, i1 = lax.top_k(s[:, :, 0], PEER_TOPK)
        s2, i2 = lax.top_k(s[:, :, 1], PEER_TOPK)
        cand = (s1[..., :, None] + s2[..., None, :]).reshape(PEER_BLOCK, PEER_HEADS, PEER_TOPK * PEER_TOPK)
        cand_id = (i1[..., :, None] * PEER_NKEYS + i2[..., None, :]).reshape(PEER_BLOCK, PEER_HEADS, PEER_TOPK * PEER_TOPK)
        top_s, pos = lax.top_k(cand, PEER_TOPK)
        ids = jnp.take_along_axis(cand_id, pos, axis=-1)
        gsc = jax.nn.softmax(top_s, axis=-1)
        act = jax.nn.gelu(jnp.einsum('td,thkd->thk', xb, u[ids]).astype(jnp.float32), approximate=False)
        coef = (gsc * act).astype(xb.dtype)
        return jnp.einsum('thk,thkd->td', coef, v[ids])
    out = lax.map(block, x.reshape(nblk, PEER_BLOCK, shp[-1]))
    return out.reshape(-1, shp[-1])[:t].reshape(shp)


def channel_sublayer(x, c, norm_g, ada_w, ada_b, w_q, sub_keys, u, v):
    sh, sc, gt = ada_params(c, ada_w, ada_b)
    return x + gt * peer_ffn(modulated_norm(x, norm_g, sh, sc), w_q, sub_keys, u, v)


def setup_inputs(seed: int = 0) -> dict:
    key = jax.random.key(seed)
    ks = iter(jax.random.split(key, 48))

    def nrm(shape, s):
        return jax.random.normal(next(ks), shape, jnp.float32) * s
    n_pages = PAST_LEN // PAGE_SIZE
    n_used = DEC_BATCH * n_pages
    n_pool = n_used + max(1, n_used // 4)
    page_table = jax.random.permutation(next(ks), n_pool)[:n_used].reshape(DEC_BATCH, n_pages).astype(jnp.int32)
    d = D_MODEL
    return {
        'x_prompt': nrm((BATCH, SEQ, d), 1.0),
        'x_sample': nrm((DEC_BATCH, DEC_SEQ, d), 1.0),
        'cache_sb_k': nrm((DEPTH, n_pool, PAGE_SIZE, SB_HEADS, D_HEAD), 1.0),
        'cache_sb_v': nrm((DEPTH, n_pool, PAGE_SIZE, SB_HEADS, D_HEAD), 1.0),
        'cache_dsa_k': nrm((DEPTH, n_pool, PAGE_SIZE, DSA_KV_HEADS, D_HEAD), 1.0),
        'cache_dsa_v': nrm((DEPTH, n_pool, PAGE_SIZE, DSA_KV_HEADS, D_HEAD), 1.0),
        'cache_idx_k': nrm((DEPTH, n_pool, PAGE_SIZE, IDX_DIM), 1.0),
        'state_ret': nrm((DEPTH, DEC_BATCH, RET_HEADS, RET_DK, RET_DV), 1.0),
        'page_table': page_table,
        'c_prompt': nrm((BATCH, d), 1.0),
        'c_sample': nrm((DEC_BATCH, d), 1.0),
        'norm1_g': 1.0 + nrm((DEPTH, d), 0.02),
        'ada1_w': nrm((DEPTH, d, 3 * d), 0.5 * d ** -0.5),
        'ada1_b': nrm((DEPTH, 3 * d), 0.02),
        'w_in': nrm((DEPTH, d, IN_WIDTH), d ** -0.5),
        'sb_qn_g': 1.0 + nrm((DEPTH, D_HEAD), 0.02),
        'sb_kn_g': 1.0 + nrm((DEPTH, D_HEAD), 0.02),
        'dsa_qn_g': 1.0 + nrm((DEPTH, D_HEAD), 0.02),
        'dsa_kn_g': 1.0 + nrm((DEPTH, D_HEAD), 0.02),
        'ret_gn_g': 1.0 + nrm((DEPTH, RET_V_W), 0.02),
        'w_br_sb': nrm((DEPTH, SB_W, d), SB_W ** -0.5),
        'w_br_ret': nrm((DEPTH, RET_V_W, d), RET_V_W ** -0.5),
        'w_br_dsa': nrm((DEPTH, DSA_W, d), DSA_W ** -0.5),
        'w_out': nrm((DEPTH, d, d), d ** -0.5),
        'norm2_g': 1.0 + nrm((DEPTH, d), 0.02),
        'ada2_w': nrm((DEPTH, d, 3 * d), 0.5 * d ** -0.5),
        'ada2_b': nrm((DEPTH, 3 * d), 0.02),
        'peer_wq': nrm((DEPTH, d, PEER_HEADS * PEER_DQ), d ** -0.5),
        'peer_subkeys': nrm((DEPTH, PEER_HEADS, 2, PEER_NKEYS, PEER_DQ // 2), (PEER_DQ // 2) ** -0.5),
        'peer_u': nrm((DEPTH, PEER_EXPERTS, d), d ** -0.5),
        'peer_v': nrm((DEPTH, PEER_EXPERTS, d), PEER_HEADS ** -0.5),
    }


def reference(x_prompt, x_sample, cache_sb_k, cache_sb_v, cache_dsa_k, cache_dsa_v, cache_idx_k, state_ret, page_table, c_prompt, c_sample, norm1_g, ada1_w, ada1_b, w_in, sb_qn_g, sb_kn_g, dsa_qn_g, dsa_kn_g, ret_gn_g, w_br_sb, w_br_ret, w_br_dsa, w_out, norm2_g, ada2_w, ada2_b, peer_wq, peer_subkeys, peer_u, peer_v):
    s_len = x_prompt.shape[1]
    n_s, t_len = x_sample.shape[:2]
    past = page_table.shape[1] * cache_sb_k.shape[2]
    pos_p = jnp.arange(s_len)
    pos_s = past + jnp.arange(t_len)
    topk_p = min(DSA_TOPK_MAX, s_len // 4)
    topk_s = min(DSA_TOPK_MAX, (past + t_len) // 4)
    log_gamma = ret_log_gamma()
    xp, xs = x_prompt, x_sample
    p_sb_k, p_sb_v, p_dsa_k, p_dsa_v, p_idx_k, p_ret = [], [], [], [], [], []
    s_sb_k, s_sb_v, s_dsa_k, s_dsa_v, s_idx_k, s_ret = [], [], [], [], [], []
    for l in range(DEPTH):
        sh, sc, gt = ada_params(c_prompt, ada1_w[l], ada1_b[l])
        m = project_mixers(modulated_norm(xp, norm1_g[l], sh, sc), w_in[l], sb_qn_g[l], sb_kn_g[l], dsa_qn_g[l], dsa_kn_g[l], pos_p)
        o_sb = sb_prompt(m['sb_q'], m['sb_k'], m['sb_v'])
        o_r, st_r = retention_prompt(m['r_q'], m['r_k'], m['r_v'], log_gamma)
        o_d = dsa_prompt(m['d_q'], m['d_k'], m['d_v'], m['i_q'], m['i_w'], m['i_k'], topk_p)
        xp = xp + gt * merge_branches(m['gates'], o_sb, retention_out(o_r, m['r_g'], ret_gn_g[l]), o_d, w_br_sb[l], w_br_ret[l], w_br_dsa[l], w_out[l])
        xp = channel_sublayer(xp, c_prompt, norm2_g[l], ada2_w[l], ada2_b[l], peer_wq[l], peer_subkeys[l], peer_u[l], peer_v[l])
        p_sb_k.append(m['sb_k'])
        p_sb_v.append(m['sb_v'])
        p_dsa_k.append(m['d_k'])
        p_dsa_v.append(m['d_v'])
        p_idx_k.append(m['i_k'])
        p_ret.append(st_r.astype(x_prompt.dtype))
        sh, sc, gt = ada_params(c_sample, ada1_w[l], ada1_b[l])
        m = project_mixers(modulated_norm(xs, norm1_g[l], sh, sc), w_in[l], sb_qn_g[l], sb_kn_g[l], dsa_qn_g[l], dsa_kn_g[l], pos_s)
        k_past = cache_sb_k[l, page_table].reshape(n_s, past, SB_HEADS, D_HEAD)
        v_past = cache_sb_v[l, page_table].reshape(n_s, past, SB_HEADS, D_HEAD)
        o_sb = sb_sample(m['sb_q'], m['sb_k'], m['sb_v'], k_past, v_past)
        o_r, st_r = retention_chunk(m['r_q'], m['r_k'], m['r_v'], state_ret[l].astype(jnp.float32), log_gamma)
        kidx_past = cache_idx_k[l, page_table].reshape(n_s, past, IDX_DIM)
        o_d = dsa_sample(m['d_q'], m['d_k'], m['d_v'], m['i_q'], m['i_w'], m['i_k'], kidx_past, cache_dsa_k, cache_dsa_v, l, page_table, topk_s)
        xs = xs + gt * merge_branches(m['gates'], o_sb, retention_out(o_r, m['r_g'], ret_gn_g[l]), o_d, w_br_sb[l], w_br_ret[l], w_br_dsa[l], w_out[l])
        xs = channel_sublayer(xs, c_sample, norm2_g[l], ada2_w[l], ada2_b[l], peer_wq[l], peer_subkeys[l], peer_u[l], peer_v[l])
        s_sb_k.append(m['sb_k'])
        s_sb_v.append(m['sb_v'])
        s_dsa_k.append(m['d_k'])
        s_dsa_v.append(m['d_v'])
        s_idx_k.append(m['i_k'])
        s_ret.append(st_r.astype(x_sample.dtype))
    return (xp, xs, jnp.stack(p_sb_k), jnp.stack(p_sb_v), jnp.stack(p_dsa_k), jnp.stack(p_dsa_v), jnp.stack(p_idx_k), jnp.stack(p_ret), jnp.stack(s_sb_k), jnp.stack(s_sb_v), jnp.stack(s_dsa_k), jnp.stack(s_dsa_v), jnp.stack(s_idx_k), jnp.stack(s_ret))
```

```python
import functools

import numpy as np
import jax
import jax.numpy as jnp
from jax import lax
from jax.experimental import pallas as pl
from jax.experimental.pallas import tpu as pltpu

F32 = jnp.float32
BF16 = jnp.bfloat16

D_HEAD = 64
SB_HEADS = 8
SB_W = SB_HEADS * D_HEAD
RET_HEADS = 4
RET_DK = 64
RET_DV = 128
RET_QK_W = RET_HEADS * RET_DK
RET_V_W = RET_HEADS * RET_DV
RET_CHUNK = 128
DSA_HEADS = 8
DSA_KV_HEADS = 2
DSA_GROUP = DSA_HEADS // DSA_KV_HEADS
DSA_W = DSA_HEADS * D_HEAD
DSA_KV_W = DSA_KV_HEADS * D_HEAD
IDX_HEADS = 4
IDX_DIM = 64
IDX_W = IDX_HEADS * IDX_DIM
DSA_TOPK_MAX = 256
Q_BLOCK = 128
N_BRANCH = 3
PEER_HEADS = 8
PEER_NKEYS = 128
PEER_EXPERTS = PEER_NKEYS * PEER_NKEYS
PEER_DQ = 256
PEER_HALF = PEER_DQ // 2
PEER_TOPK = 16
RMS_EPS = 1e-6
GN_EPS = 1e-5
ROPE_BASE = 10000.0

LANES = 128
VMEM_LIMIT = 56 * 1024 * 1024
INT_MIN = np.int32(-2 ** 31)
NEG_INF = float("-inf")


def _bf(x):
    return x.astype(BF16)


def _split(x):
    hi = x.astype(BF16)
    lo = (x - hi.astype(F32)).astype(BF16)
    return hi, lo


def _dot(a, b):
    return jnp.dot(a, b, preferred_element_type=F32)


def _dot_nt(a, b):
    return lax.dot_general(a, b, (((1,), (1,)), ((), ())), preferred_element_type=F32)


def _dot_tn(a, b):
    return lax.dot_general(a, b, (((0,), (0,)), ((), ())), preferred_element_type=F32)


def _dot3_nt(a, b):
    ah, al = _split(a)
    bh, bl = _split(b)
    return _dot_nt(ah, bh) + _dot_nt(ah, bl) + _dot_nt(al, bh)


def _sigmoid(x):
    return 1.0 / (1.0 + jnp.exp(-x))


def _params(sem):
    return pltpu.CompilerParams(dimension_semantics=sem, vmem_limit_bytes=VMEM_LIMIT)


def _full_spec(shape):
    nd = len(shape)
    return pl.BlockSpec(shape, lambda *_: (0,) * nd, pipeline_mode=pl.Buffered(1))


def _mod_operand(mod, n_tok, tb):
    g, d = mod.shape
    tg = n_tok // g
    if tg % tb == 0:
        per = tg // tb
        return mod.reshape(g, 1, d), pl.BlockSpec((None, 1, d), lambda i, *_: (i // per, 0, 0))
    return jnp.repeat(mod, tg, axis=0), pl.BlockSpec((tb, d), lambda i, *_: (i, 0))


def _rms_mod(x, g, shift, scale):
    ms = jnp.mean(x * x, axis=-1, keepdims=True)
    return (x * lax.rsqrt(ms + RMS_EPS) * g) * (1.0 + scale) + shift


def _sortable_key(score):
    bits = lax.bitcast_convert_type(score, jnp.int32)
    return jnp.where(bits < 0, bits ^ np.int32(0x7FFFFFFF), bits)


def _kth_largest_key(key, k):
    rows = key.shape[0]

    def body(b, ans):
        cand = ans | jnp.left_shift(jnp.int32(1), 31 - b)
        cnt = jnp.sum(jnp.where(key >= (cand ^ INT_MIN), 1.0, 0.0), axis=1, keepdims=True)
        return jnp.where(cnt >= k, cand, ans)

    ans = lax.fori_loop(0, 32, body, jnp.zeros((rows, 1), jnp.int32))
    return ans ^ INT_MIN


def _topk_mask(score, k, incl):
    key = _sortable_key(score)
    thr = _kth_largest_key(key, k)
    gt = key > thr
    tie = key == thr
    need = k - jnp.sum(jnp.where(gt, 1.0, 0.0), axis=1, keepdims=True)
    out = []
    off = jnp.zeros_like(need)
    for c in range(score.shape[1] // LANES):
        sl = slice(c * LANES, (c + 1) * LANES)
        tie_c = tie[:, sl]
        rank = _dot(jnp.where(tie_c, 1.0, 0.0).astype(BF16), incl) + off
        out.append(jnp.where(gt[:, sl] | (tie_c & (rank <= need)), 1.0, 0.0))
        off = rank[:, LANES - 1:LANES]
    return jnp.concatenate(out, axis=1)


def _stick_tile(z, valid, tmat, tot):
    l1p = jnp.log1p(jnp.exp(-jnp.abs(z)))
    ls = jnp.minimum(z, 0.0) - l1p
    lk = -jnp.maximum(z, 0.0) - l1p
    if valid is not None:
        lk = jnp.where(valid, lk, 0.0)
    hi, lo = _split(lk)
    mm = _dot(hi, tmat) + _dot(lo, tmat)
    a = jnp.exp(ls + mm[:, :LANES] + tot)
    if valid is not None:
        a = jnp.where(valid, a, 0.0)
    return a, tot + mm[:, LANES:]


def _ada_kernel(c_ref, w_ref, b_ref, o_ref):
    c = c_ref[...]
    o_ref[...] = _dot(_bf(c * _sigmoid(c)), _bf(w_ref[...])) + b_ref[...]


def _ada(c, w, b):
    n, d = c.shape
    nl, _, w3 = w.shape
    tn = 1024
    return pl.pallas_call(
        _ada_kernel,
        grid=(nl, w3 // tn),
        in_specs=[pl.BlockSpec((n, d), lambda l, j: (0, 0)),
                  pl.BlockSpec((None, d, tn), lambda l, j: (l, 0, j)),
                  pl.BlockSpec((None, 1, tn), lambda l, j: (l, 0, j))],
        out_specs=pl.BlockSpec((None, n, tn), lambda l, j: (l, 0, j)),
        out_shape=jax.ShapeDtypeStruct((nl, n, w3), F32),
        compiler_params=_params(("arbitrary", "arbitrary")),
        name="ada",
    )(c, w, b.reshape(nl, 1, w3))


_C_SBQ, _C_SBK, _C_SBV = 0, 512, 1024
_C_RQ, _C_RK, _C_RV, _C_RG = 1536, 1792, 2048, 2560
_C_DQ, _C_DK, _C_DV, _C_IQ = 3072, 3584, 3712, 3840
_C_IK, _C_IW, _C_G = 4096, 4160, 4224
_W_PACKED = _C_G + N_BRANCH * 1024


def _pack_w_in(w, d_model):
    assert _W_PACKED == _C_G + N_BRANCH * d_model
    a = w[:, :4096]
    iw = w[:, 4096:4100]
    ik = w[:, 4100:4164]
    g = w[:, 4164:]
    pad = jnp.zeros((w.shape[0], _C_G - _C_IW - IDX_HEADS), w.dtype)
    return jnp.concatenate([a, ik, iw, pad, g], axis=1).astype(BF16)


def _head_rms(y, bd, g):
    hi, lo = _split(y * y)
    ms = (_dot(hi, bd) + _dot(lo, bd)) * (1.0 / D_HEAD)
    return y * lax.rsqrt(ms + RMS_EPS) * g


def _rotary(y, cos, sin, first_half):
    w = y.shape[1]
    partner = jnp.where(first_half, pltpu.roll(y, w - D_HEAD // 2, 1), pltpu.roll(y, D_HEAD // 2, 1))
    return y * cos + partner * sin


def _inproj_kernel(x_ref, sh_ref, sc_ref, ng_ref, w_ref, bd_ref, gq_ref, gk_ref, gdq_ref, gdk_ref,
                   cos_ref, sin_ref,
                   sbq_hm, sbk_o, sbv_o, sbk_hm, sbv_hm, rq_o, rk_o, rv_o, rg_o,
                   dq_hm, dk_o, dv_o, dk_hm, dv_hm, iq_o, ik_o, iw_o, g_o):
    h = _bf(_rms_mod(x_ref[...], ng_ref[...], sh_ref[...], sc_ref[...]))

    def proj(c0, width):
        return _dot(h, w_ref[:, c0:c0 + width])

    def to_heads(y, ref, nh):
        for i in range(nh):
            ref[i] = y[:, i * D_HEAD:(i + 1) * D_HEAD].astype(ref.dtype)

    bd = bd_ref[...]
    to_heads(_head_rms(proj(_C_SBQ, SB_W), bd, gq_ref[...]), sbq_hm, SB_HEADS)
    sbk = _head_rms(proj(_C_SBK, SB_W), bd, gk_ref[...])
    sbk_o[...] = sbk
    to_heads(sbk, sbk_hm, SB_HEADS)
    sbv = proj(_C_SBV, SB_W)
    sbv_o[...] = sbv
    to_heads(sbv, sbv_hm, SB_HEADS)

    cos = cos_ref[...]
    sin = sin_ref[...]
    lane = lax.broadcasted_iota(jnp.int32, cos.shape, 1)
    first_half = (lane % D_HEAD) < (D_HEAD // 2)
    rq_o[...] = _rotary(proj(_C_RQ, RET_QK_W), cos, sin, first_half)
    rk_o[...] = _rotary(proj(_C_RK, RET_QK_W), cos, sin, first_half) * (RET_DK ** -0.5)
    rv_o[...] = proj(_C_RV, RET_V_W)
    rg_o[...] = proj(_C_RG, RET_V_W)

    to_heads(_head_rms(proj(_C_DQ, DSA_W), bd, gdq_ref[...]), dq_hm, DSA_HEADS)
    dk = _head_rms(proj(_C_DK, DSA_KV_W), bd_ref[:DSA_KV_W, :DSA_KV_W], gdk_ref[...])
    dk_o[...] = dk
    to_heads(dk, dk_hm, DSA_KV_HEADS)
    dv = proj(_C_DV, DSA_KV_W)
    dv_o[...] = dv
    to_heads(dv, dv_hm, DSA_KV_HEADS)

    iq_o[...] = proj(_C_IQ, IDX_W)
    ikw = proj(_C_IK, LANES)
    ik_o[...] = ikw[:, :IDX_DIM]
    iw_o[...] = ikw[:, IDX_DIM:IDX_DIM + IDX_HEADS] * (IDX_HEADS ** -0.5)
    g_o[...] = _sigmoid(proj(_C_G, g_o.shape[1]))


def _inproj(x, shift, scale, norm_g, w_packed, bd, gq, gk, gdq, gdk, cos, sin, tb):
    t, d = x.shape
    nblk = t // tb
    rep = cos.shape[0] // tb
    sh_arr, sh_spec = _mod_operand(shift, t, tb)
    sc_arr, sc_spec = _mod_operand(scale, t, tb)

    def tok(width):
        return pl.BlockSpec((tb, width), lambda i: (i, 0))

    def hm(nh):
        return pl.BlockSpec((nh, tb, D_HEAD), lambda i: (0, i, 0))

    def sds(shape, dt=F32):
        return jax.ShapeDtypeStruct(shape, dt)

    out_specs = [hm(8), tok(512), tok(512), hm(8), hm(8), tok(256), tok(256), tok(512), tok(512),
                 hm(8), tok(128), tok(128), hm(2), hm(2), tok(256), tok(64), tok(4), tok(3 * d)]
    out_shape = [sds((8, t, 64), BF16), sds((t, 512)), sds((t, 512)), sds((8, t, 64), BF16), sds((8, t, 64), BF16),
                 sds((t, 256)), sds((t, 256)), sds((t, 512)), sds((t, 512)),
                 sds((8, t, 64), BF16), sds((t, 128)), sds((t, 128)), sds((2, t, 64), BF16), sds((2, t, 64), BF16),
                 sds((t, 256)), sds((t, 64)), sds((t, 4)), sds((t, 3 * d))]
    names = ["sbq_hm", "sbk", "sbv", "sbk_hm", "sbv_hm", "rq", "rk", "rv", "rg",
             "dq_hm", "dk", "dv", "dk_hm", "dv_hm", "iq", "ik", "iw", "g"]
    outs = pl.pallas_call(
        _inproj_kernel,
        grid=(nblk,),
        in_specs=[tok(d), sh_spec, sc_spec, _full_spec((1, d)), _full_spec(w_packed.shape), _full_spec(bd.shape),
                  _full_spec((1, 512)), _full_spec((1, 512)), _full_spec((1, 512)), _full_spec((1, 128)),
                  pl.BlockSpec((tb, 256), lambda i: (i % rep, 0)), pl.BlockSpec((tb, 256), lambda i: (i % rep, 0))],
        out_specs=out_specs,
        out_shape=out_shape,
        compiler_params=_params(("arbitrary",)),
        name="inproj",
    )(x, sh_arr, sc_arr, norm_g, w_packed, bd, gq, gk, gdq, gdk, cos, sin)
    return dict(zip(names, outs))


def _sbp_kernel(q_ref, k_ref, v_ref, t_ref, o_ref):
    i = pl.program_id(2)
    q = q_ref[...]
    tmat = t_ref[...]
    row = lax.broadcasted_iota(jnp.int32, (Q_BLOCK, LANES), 0)
    col = lax.broadcasted_iota(jnp.int32, (Q_BLOCK, LANES), 1)

    def tile(j, carry, valid):
        acc, tot = carry
        off = pl.multiple_of(j * LANES, LANES)
        z = _dot_nt(q, k_ref[pl.ds(off, LANES), :]) * (D_HEAD ** -0.5)
        a, tot = _stick_tile(z, valid, tmat, tot)
        return acc + _dot(_bf(a), v_ref[pl.ds(off, LANES), :]), tot

    carry = (jnp.zeros((Q_BLOCK, D_HEAD), F32), jnp.zeros((Q_BLOCK, LANES), F32))
    carry = tile(i, carry, col < row)
    carry = lax.fori_loop(0, i, lambda jj, c: tile(i - 1 - jj, c, None), carry)
    o_ref[...] = carry[0]


def _sb_prompt(q_hm, k_hm, v_hm, tmat, nb, s):
    h, t, _ = q_hm.shape
    nq = s // Q_BLOCK
    return pl.pallas_call(
        _sbp_kernel,
        grid=(nb, h, nq),
        in_specs=[pl.BlockSpec((None, Q_BLOCK, D_HEAD), lambda b, hh, i: (hh, b * nq + i, 0)),
                  pl.BlockSpec((None, s, D_HEAD), lambda b, hh, i: (hh, b, 0)),
                  pl.BlockSpec((None, s, D_HEAD), lambda b, hh, i: (hh, b, 0)),
                  _full_spec(tmat.shape)],
        out_specs=pl.BlockSpec((None, Q_BLOCK, D_HEAD), lambda b, hh, i: (hh, b * nq + i, 0)),
        out_shape=jax.ShapeDtypeStruct((h, t, D_HEAD), F32),
        compiler_params=_params(("arbitrary", "arbitrary", "arbitrary")),
        name="sb_prompt",
    )(q_hm, k_hm, v_hm, tmat)


def _ret_log_gamma():
    return np.log1p(-np.exp2(-5.0 - np.arange(RET_HEADS, dtype=np.float32))).astype(np.float32)


def _ret_kernel(q_ref, k_ref, v_ref, g_ref, s0_ref, dec_ref, qd_ref, kd_ref, gn_ref, o_ref, s1_ref, s_scr, *, sdecay):
    c = pl.program_id(1)

    @pl.when(c == 0)
    def _():
        s_scr[...] = s0_ref[...]

    q = q_ref[...]
    k = k_ref[...]
    v = v_ref[...]
    outs = []
    for h in range(RET_HEADS):
        qh = _bf(q[:, h * RET_DK:(h + 1) * RET_DK])
        kh = k[:, h * RET_DK:(h + 1) * RET_DK]
        vh = _bf(v[:, h * RET_DV:(h + 1) * RET_DV])
        s = s_scr[h]
        inner = _dot_nt(qh, _bf(kh)) * dec_ref[h]
        o = _dot(_bf(inner), vh) + _dot(qh, _bf(s)) * qd_ref[h]
        s_scr[h] = sdecay[h] * s + _dot_tn(_bf(kh * kd_ref[h]), vh)
        mu = jnp.mean(o, axis=-1, keepdims=True)
        var = jnp.mean(jnp.square(o - mu), axis=-1, keepdims=True)
        outs.append((o - mu) * lax.rsqrt(var + GN_EPS))
    on = jnp.concatenate(outs, axis=1) * gn_ref[...]
    rg = g_ref[...]
    o_ref[...] = (rg * _sigmoid(rg)) * on

    @pl.when(c == pl.num_programs(1) - 1)
    def _():
        s1_ref[...] = s_scr[...]


def _retention(q, k, v, rg, s0, gn_g, c_eff):
    n, s, _ = q.shape
    cc = RET_CHUNK
    lg = _ret_log_gamma().astype(np.float64)
    idx = np.arange(cc, dtype=np.float64)
    diff = idx[:, None] - idx[None, :]
    dec = np.where(diff[None] >= 0, np.exp(np.maximum(diff, 0.0)[None] * lg[:, None, None]), 0.0)
    qd = np.exp((idx + 1.0)[None, :, None] * lg[:, None, None])
    kd = np.exp((c_eff - 1.0 - idx)[None, :, None] * lg[:, None, None])
    kd = np.where(idx[None, :, None] < c_eff, kd, 0.0)
    sdecay = tuple(float(x) for x in np.exp(c_eff * lg))

    def tok(width):
        return pl.BlockSpec((None, cc, width), lambda i, c: (i, c, 0))

    st = pl.BlockSpec((None, RET_HEADS, RET_DK, RET_DV), lambda i, c: (i, 0, 0, 0))
    return pl.pallas_call(
        functools.partial(_ret_kernel, sdecay=sdecay),
        grid=(n, s // cc),
        in_specs=[tok(256), tok(256), tok(512), tok(512), st,
                  _full_spec((RET_HEADS, cc, cc)), _full_spec((RET_HEADS, cc, 1)), _full_spec((RET_HEADS, cc, 1)),
                  _full_spec((1, RET_V_W))],
        out_specs=[tok(512), st],
        out_shape=[jax.ShapeDtypeStruct((n, s, RET_V_W), F32),
                   jax.ShapeDtypeStruct((n, RET_HEADS, RET_DK, RET_DV), F32)],
        scratch_shapes=[pltpu.VMEM((RET_HEADS, RET_DK, RET_DV), F32)],
        compiler_params=_params(("arbitrary", "arbitrary")),
        name="retention",
    )(q, k, v, rg, s0, jnp.asarray(dec, F32), jnp.asarray(qd, F32), jnp.asarray(kd, F32), gn_g)


def _dsap_kernel(iq_ref, iw_ref, ik_ref, q_ref, k_ref, v_ref, incl_ref, o_ref, *, topk):
    i = pl.program_id(1)
    s = ik_ref.shape[0]
    iq = iq_ref[...]
    iw = iw_ref[...]
    ik = ik_ref[...]
    score = jnp.zeros((Q_BLOCK, s), F32)
    for h in range(IDX_HEADS):
        dots = _dot3_nt(iq[:, h * IDX_DIM:(h + 1) * IDX_DIM], ik) * (IDX_DIM ** -0.5)
        score = score + iw[:, h:h + 1] * jnp.maximum(dots, 0.0)
    qpos = i * Q_BLOCK + lax.broadcasted_iota(jnp.int32, (Q_BLOCK, s), 0)
    kpos = lax.broadcasted_iota(jnp.int32, (Q_BLOCK, s), 1)
    causal = kpos <= qpos
    score = jnp.where(causal, score + 0.0, NEG_INF)
    sel = (_topk_mask(score, topk, incl_ref[...]) > 0.5) & causal
    for g in range(DSA_KV_HEADS):
        kg = k_ref[g]
        vg = v_ref[g]
        for j in range(DSA_GROUP):
            hh = g * DSA_GROUP + j
            logits = jnp.where(sel, _dot_nt(q_ref[hh], kg) * (D_HEAD ** -0.5), NEG_INF)
            m = jnp.max(logits, axis=-1, keepdims=True)
            p = jnp.exp(logits - m)
            p = p / jnp.sum(p, axis=-1, keepdims=True)
            o_ref[hh] = _dot(_bf(p), vg)


def _dsa_prompt(iq, iw, ik, q_hm, k_hm, v_hm, incl, nb, s, topk):
    t = iq.shape[0]
    nq = s // Q_BLOCK
    return pl.pallas_call(
        functools.partial(_dsap_kernel, topk=topk),
        grid=(nb, nq),
        in_specs=[pl.BlockSpec((Q_BLOCK, IDX_W), lambda b, i: (b * nq + i, 0)),
                  pl.BlockSpec((Q_BLOCK, IDX_HEADS), lambda b, i: (b * nq + i, 0)),
                  pl.BlockSpec((s, IDX_DIM), lambda b, i: (b, 0)),
                  pl.BlockSpec((DSA_HEADS, Q_BLOCK, D_HEAD), lambda b, i: (0, b * nq + i, 0)),
                  pl.BlockSpec((DSA_KV_HEADS, s, D_HEAD), lambda b, i: (0, b, 0)),
                  pl.BlockSpec((DSA_KV_HEADS, s, D_HEAD), lambda b, i: (0, b, 0)),
                  _full_spec(incl.shape)],
        out_specs=pl.BlockSpec((DSA_HEADS, Q_BLOCK, D_HEAD), lambda b, i: (0, b * nq + i, 0)),
        out_shape=jax.ShapeDtypeStruct((DSA_HEADS, t, D_HEAD), F32),
        compiler_params=_params(("arbitrary", "arbitrary")),
        name="dsa_prompt",
    )(iq, iw, ik, q_hm, k_hm, v_hm, incl)


def _merge_kernel(osb_ref, oret_ref, odsa_ref, g_ref, x_ref, gt_ref, wsb_ref, wret_ref, wdsa_ref, wout_ref, o_ref):
    d = x_ref.shape[1]
    osb = _bf(jnp.concatenate([osb_ref[h] for h in range(SB_HEADS)], axis=1))
    odsa = _bf(jnp.concatenate([odsa_ref[h] for h in range(DSA_HEADS)], axis=1))
    merged = (g_ref[:, :d] * _dot(osb, wsb_ref[...])
              + g_ref[:, d:2 * d] * _dot(_bf(oret_ref[...]), wret_ref[...])
              + g_ref[:, 2 * d:] * _dot(odsa, wdsa_ref[...]))
    o_ref[...] = x_ref[...] + gt_ref[...] * _dot(_bf(merged), wout_ref[...])


def _merge(osb_hm, oret, odsa_hm, g, x, gate, wsb, wret, wdsa, wout, tb):
    t, d = x.shape
    gt_arr, gt_spec = _mod_operand(gate, t, tb)

    def tok(width):
        return pl.BlockSpec((tb, width), lambda i: (i, 0))

    hm = pl.BlockSpec((8, tb, D_HEAD), lambda i: (0, i, 0))
    return pl.pallas_call(
        _merge_kernel,
        grid=(t // tb,),
        in_specs=[hm, tok(512), hm, tok(3 * d), tok(d), gt_spec,
                  _full_spec(wsb.shape), _full_spec(wret.shape), _full_spec(wdsa.shape), _full_spec(wout.shape)],
        out_specs=tok(d),
        out_shape=jax.ShapeDtypeStruct((t, d), F32),
        compiler_params=_params(("arbitrary",)),
        name="merge",
    )(osb_hm, oret, odsa_hm, g, x, gt_arr, wsb, wret, wdsa, wout)


def _top_vals(s, n):
    vals = []
    for _ in range(n):
        m = jnp.max(s, axis=0, keepdims=True)
        vals.append(m)
        s = jnp.where(s == m, NEG_INF, s)
    return vals


def _peer1_kernel(x_ref, sh_ref, sc_ref, ng_ref, wq_ref, keys_ref, h2t_ref, th_ref, e1_ref, s2_ref, e2_ref, q_scr):
    h = pl.program_id(1)

    @pl.when(h == 0)
    def _():
        h2 = _rms_mod(x_ref[...], ng_ref[...], sh_ref[...], sc_ref[...])
        h2t_ref[...] = _bf(h2.T)
        q = _dot(_bf(h2), wq_ref[...])
        for c in range(2 * PEER_HEADS):
            q_scr[c] = q[:, c * PEER_HALF:(c + 1) * PEER_HALF]

    s1 = _dot3_nt(keys_ref[0], q_scr[2 * h])
    s2 = _dot3_nt(keys_ref[1], q_scr[2 * h + 1])
    t1 = _top_vals(s1, PEER_TOPK)
    t2 = jnp.concatenate(_top_vals(s2, PEER_TOPK), axis=0)
    cand = jnp.concatenate([a + t2 for a in t1], axis=0)
    best = _top_vals(cand, PEER_TOPK + 1)
    z = jnp.zeros_like(best[0])
    for b in best[:PEER_TOPK]:
        z = z + jnp.exp(b - best[0])
    thr = 0.5 * (best[PEER_TOPK - 1] + best[PEER_TOPK])
    th_ref[...] = thr - s1
    e1_ref[...] = jnp.exp(s1 - t1[0]) / z
    s2_ref[...] = s2
    e2_ref[...] = jnp.exp(s2 - t2[0:1])


def _peer_scores(x, shift, scale, norm_g, wq, keys, tb):
    t, d = x.shape
    sh_arr, sh_spec = _mod_operand(shift, t, tb)
    sc_arr, sc_spec = _mod_operand(scale, t, tb)
    per_head = pl.BlockSpec((None, PEER_NKEYS, tb), lambda i, h: (h, 0, i))
    sds = jax.ShapeDtypeStruct((PEER_HEADS, PEER_NKEYS, t), F32)
    return pl.pallas_call(
        _peer1_kernel,
        grid=(t // tb, PEER_HEADS),
        in_specs=[pl.BlockSpec((tb, d), lambda i, h: (i, 0)), sh_spec, sc_spec, _full_spec((1, d)),
                  _full_spec(wq.shape),
                  pl.BlockSpec((None, 2, PEER_NKEYS, PEER_HALF), lambda i, h: (h, 0, 0, 0))],
        out_specs=[pl.BlockSpec((d, tb), lambda i, h: (0, i)), per_head, per_head, per_head, per_head],
        out_shape=[jax.ShapeDtypeStruct((d, t), BF16), sds, sds, sds, sds],
        scratch_shapes=[pltpu.VMEM((2 * PEER_HEADS, tb, PEER_HALF), F32)],
        compiler_params=_params(("arbitrary", "arbitrary")),
        name="peer_scores",
    )(x, sh_arr, sc_arr, norm_g, wq, keys)


def _peer2_kernel(h2t_ref, u_ref, vt_ref, th_ref, e1_ref, s2_ref, e2_ref, x_ref, gt_ref, o_ref,
                  acc_ref, act_ref, c_ref):
    j = pl.program_id(1)
    tb = x_ref.shape[0]
    ch = u_ref.shape[0]

    @pl.when(j == 0)
    def _():
        acc_ref[...] = jnp.zeros_like(acc_ref)

    act_ref[...] = _dot(u_ref[...], h2t_ref[...])

    def body(r, carry):
        ro = pl.multiple_of(r * PEER_NKEYS, PEER_NKEYS)
        for cs in range(tb // LANES):
            sl = slice(cs * LANES, (cs + 1) * LANES)
            w = jnp.zeros((PEER_NKEYS, LANES), F32)
            for h in range(PEER_HEADS):
                sel = s2_ref[h, :, sl] >= th_ref[h, r, :, sl]
                w = w + jnp.where(sel, e2_ref[h, :, sl], 0.0) * e1_ref[h, r, :, sl]
            a = act_ref[pl.ds(ro, PEER_NKEYS), sl]
            gelu = 0.5 * a * (1.0 + lax.erf(a * (2.0 ** -0.5)))
            c_ref[pl.ds(ro, PEER_NKEYS), sl] = _bf(gelu * w)
        return carry

    lax.fori_loop(0, ch // PEER_NKEYS, body, 0)
    acc_ref[...] += _dot(vt_ref[...], c_ref[...])

    @pl.when(j == pl.num_programs(1) - 1)
    def _():
        o_ref[...] = x_ref[...] + gt_ref[...] * acc_ref[...].T


def _peer_experts(h2t, u, vt, th, e1, s2, e2, x, gate, tb, ch):
    t, d = x.shape
    ne = u.shape[0]
    rows = ch // PEER_NKEYS
    gt_arr, gt_spec = _mod_operand(gate, t, tb)
    by_row = pl.BlockSpec((PEER_HEADS, rows, 1, tb), lambda i, j: (0, j, 0, i))
    th = th.reshape(PEER_HEADS, PEER_NKEYS, 1, t)
    e1 = e1.reshape(PEER_HEADS, PEER_NKEYS, 1, t)
    by_tok = pl.BlockSpec((PEER_HEADS, PEER_NKEYS, tb), lambda i, j: (0, 0, i))
    return pl.pallas_call(
        _peer2_kernel,
        grid=(t // tb, ne // ch),
        in_specs=[pl.BlockSpec((d, tb), lambda i, j: (0, i)),
                  pl.BlockSpec((ch, d), lambda i, j: (j, 0)),
                  pl.BlockSpec((d, ch), lambda i, j: (0, j)),
                  by_row, by_row, by_tok, by_tok,
                  pl.BlockSpec((tb, d), lambda i, j: (i, 0)), gt_spec],
        out_specs=pl.BlockSpec((tb, d), lambda i, j: (i, 0)),
        out_shape=jax.ShapeDtypeStruct((t, d), F32),
        scratch_shapes=[pltpu.VMEM((d, tb), F32), pltpu.VMEM((ch, tb), F32), pltpu.VMEM((ch, tb), BF16)],
        compiler_params=_params(("arbitrary", "arbitrary")),
        name="peer_experts",
    )(h2t, u, vt, th, e1, s2, e2, x, gt_arr)


def _page_of(n, s, pt, npages):
    return pt[n, npages - jnp.maximum(s, 1)]


def _pos_block(s, npages):
    return jnp.where(s == 0, npages, npages - jnp.maximum(s, 1))


def _samp1_kernel(pt_ref, qbd_ref, kc_ref, vc_ref, kn_ref, vn_ref, ic_ref, in_ref, iq_ref, iw_ref, t_ref,
                  o_ref, sc_ref, acc_ref, tot_ref):
    s = pl.program_id(1)
    nrow = qbd_ref.shape[0]

    @pl.when(s == 0)
    def _():
        acc_ref[...] = jnp.zeros_like(acc_ref)
        tot_ref[...] = jnp.zeros_like(tot_ref)

    is_new = s == 0
    k = jnp.where(is_new, kn_ref[...], kc_ref[...])
    v = jnp.where(is_new, vn_ref[...], vc_ref[...])
    ik = jnp.where(is_new, in_ref[...], ic_ref[...])

    z = _dot_nt(qbd_ref[...], _bf(k)) * (D_HEAD ** -0.5)
    trow = lax.broadcasted_iota(jnp.int32, (nrow, LANES), 0) // SB_HEADS
    col = lax.broadcasted_iota(jnp.int32, (nrow, LANES), 1)
    valid = jnp.logical_or(s > 0, col < trow)
    a, tot = _stick_tile(z, valid, t_ref[...], tot_ref[...])
    tot_ref[...] = tot
    acc_ref[...] += _dot(_bf(a), _bf(v))

    nt = sc_ref.shape[0]
    sc = jnp.maximum(_dot3_nt(iq_ref[...], ik) * (IDX_DIM ** -0.5), 0.0) * iw_ref[...]
    score = sc[0:nt]
    for h in range(1, IDX_HEADS):
        score = score + sc[h * nt:(h + 1) * nt]
    sc_ref[...] = score

    @pl.when(s == pl.num_programs(1) - 1)
    def _():
        acc = acc_ref[...]
        hrow = lax.broadcasted_iota(jnp.int32, acc.shape, 0) % SB_HEADS
        hcol = lax.broadcasted_iota(jnp.int32, acc.shape, 1) // D_HEAD
        own = jnp.where(hrow == hcol, acc, 0.0)
        o_ref[...] = jnp.sum(own.reshape(nrow // SB_HEADS, SB_HEADS, acc.shape[1]), axis=1)


def _sample_sb_and_scores(page_table, qbd, cache_k, cache_v, k_new, v_new, cache_ik, ik_new, iq16, iw16, tmat, layer):
    n, npages = page_table.shape
    nrow = qbd.shape[1]
    nt = nrow // SB_HEADS
    page = PAGE = cache_k.shape[2]
    assert page == LANES

    def cache_spec(width):
        return pl.BlockSpec((None, None, PAGE, width), lambda i, s, pt: (layer, _page_of(i, s, pt, npages), 0, 0))

    def per_n(rows, width):
        return pl.BlockSpec((None, rows, width), lambda i, s, pt: (i, 0, 0))

    grid_spec = pltpu.PrefetchScalarGridSpec(
        num_scalar_prefetch=1,
        grid=(n, npages + 1),
        in_specs=[per_n(nrow, SB_W), cache_spec(SB_W), cache_spec(SB_W), per_n(PAGE, SB_W), per_n(PAGE, SB_W),
                  cache_spec(IDX_DIM), per_n(PAGE, IDX_DIM), per_n(IDX_HEADS * nt, IDX_DIM), per_n(IDX_HEADS * nt, 1),
                  pl.BlockSpec(tmat.shape, lambda i, s, pt: (0, 0))],
        out_specs=[per_n(nt, SB_W),
                   pl.BlockSpec((None, nt, LANES), lambda i, s, pt: (i, 0, _pos_block(s, npages)))],
        scratch_shapes=[pltpu.VMEM((nrow, SB_W), F32), pltpu.VMEM((nrow, LANES), F32)],
    )
    return pl.pallas_call(
        _samp1_kernel,
        grid_spec=grid_spec,
        out_shape=[jax.ShapeDtypeStruct((n, nt, SB_W), F32),
                   jax.ShapeDtypeStruct((n, nt, (npages + 1) * LANES), F32)],
        compiler_params=_params(("arbitrary", "arbitrary")),
        name="sample_sb",
    )(page_table, qbd, cache_k, cache_v, k_new, v_new, cache_ik, ik_new, iq16, iw16, tmat)


def _sthr_kernel(sc_ref, incl_ref, m_ref, *, topk, past, nt):
    rows, width = sc_ref.shape
    t = (pl.program_id(0) * rows + lax.broadcasted_iota(jnp.int32, (rows, width), 0)) % nt
    pos = lax.broadcasted_iota(jnp.int32, (rows, width), 1)
    causal = pos <= past + t
    score = jnp.where(causal, sc_ref[...] + 0.0, NEG_INF)
    m_ref[...] = jnp.where(causal, _topk_mask(score, topk, incl_ref[...]), 0.0)


def _sample_select(scores, incl, topk, past, nt):
    r, width = scores.shape
    rb = 8
    return pl.pallas_call(
        functools.partial(_sthr_kernel, topk=topk, past=past, nt=nt),
        grid=(r // rb,),
        in_specs=[pl.BlockSpec((rb, width), lambda i: (i, 0)), _full_spec(incl.shape)],
        out_specs=pl.BlockSpec((rb, width), lambda i: (i, 0)),
        out_shape=jax.ShapeDtypeStruct((r, width), F32),
        compiler_params=_params(("arbitrary",)),
        name="sample_select",
    )(scores, incl)


def _samp2_kernel(pt_ref, q_ref, kc_ref, vc_ref, kn_ref, vn_ref, m_ref, o_ref, mx_ref, l_ref, acc_ref):
    s = pl.program_id(1)

    @pl.when(s == 0)
    def _():
        mx_ref[...] = jnp.full_like(mx_ref, -1e30)
        l_ref[...] = jnp.zeros_like(l_ref)
        acc_ref[...] = jnp.zeros_like(acc_ref)

    is_new = s == 0
    k = _bf(jnp.where(is_new, kn_ref[...], kc_ref[...]))
    v = _bf(jnp.where(is_new, vn_ref[...], vc_ref[...]))
    m8 = m_ref[...]
    sel = jnp.concatenate([m8] * (q_ref.shape[1] // m8.shape[0]), axis=0) > 0.5
    for g in range(DSA_KV_HEADS):
        logits = _dot_nt(q_ref[g], k[:, g * D_HEAD:(g + 1) * D_HEAD]) * (D_HEAD ** -0.5)
        m_old = mx_ref[g]
        m_new = jnp.maximum(m_old, jnp.max(jnp.where(sel, logits, -1e30), axis=-1, keepdims=True))
        p = jnp.where(sel, jnp.exp(logits - m_new), 0.0)
        alpha = jnp.exp(m_old - m_new)
        mx_ref[g] = m_new
        l_ref[g] = alpha * l_ref[g] + jnp.sum(p, axis=-1, keepdims=True)
        acc_ref[g] = alpha * acc_ref[g] + _dot(_bf(p), v[:, g * D_HEAD:(g + 1) * D_HEAD])

    @pl.when(s == pl.num_programs(1) - 1)
    def _():
        o_ref[...] = acc_ref[...] / l_ref[...]


def _sample_dsa(page_table, q, cache_k, cache_v, k_new, v_new, mask8, layer):
    n, npages = page_table.shape
    rows = q.shape[2]
    page = cache_k.shape[2]

    def cache_spec():
        return pl.BlockSpec((None, None, page, DSA_KV_W), lambda i, s, pt: (layer, _page_of(i, s, pt, npages), 0, 0))

    new_spec = pl.BlockSpec((None, page, DSA_KV_W), lambda i, s, pt: (i, 0, 0))
    grid_spec = pltpu.PrefetchScalarGridSpec(
        num_scalar_prefetch=1,
        grid=(n, npages + 1),
        in_specs=[pl.BlockSpec((None, DSA_KV_HEADS, rows, D_HEAD), lambda i, s, pt: (i, 0, 0, 0)),
                  cache_spec(), cache_spec(), new_spec, new_spec,
                  pl.BlockSpec((None, 8, LANES), lambda i, s, pt: (i, 0, _pos_block(s, npages)))],
        out_specs=pl.BlockSpec((None, DSA_KV_HEADS, rows, D_HEAD), lambda i, s, pt: (i, 0, 0, 0)),
        scratch_shapes=[pltpu.VMEM((DSA_KV_HEADS, rows, 1), F32), pltpu.VMEM((DSA_KV_HEADS, rows, 1), F32),
                        pltpu.VMEM((DSA_KV_HEADS, rows, D_HEAD), F32)],
    )
    return pl.pallas_call(
        _samp2_kernel,
        grid_spec=grid_spec,
        out_shape=jax.ShapeDtypeStruct((n, DSA_KV_HEADS, rows, D_HEAD), F32),
        compiler_params=_params(("arbitrary", "arbitrary")),
        name="sample_dsa",
    )(page_table, q, cache_k, cache_v, k_new, v_new, mask8)


def _rope_tables(pos):
    half = D_HEAD // 2
    freqs = ROPE_BASE ** (-jnp.arange(half, dtype=F32) / half)
    ang = pos.astype(F32)[:, None] * freqs[None, :]
    cos = jnp.cos(ang)
    sin = jnp.sin(ang)
    return (jnp.tile(jnp.concatenate([cos, cos], axis=1), (1, RET_HEADS)),
            jnp.tile(jnp.concatenate([-sin, sin], axis=1), (1, RET_HEADS)))


def _tri_constants():
    j = np.arange(LANES)
    suffix = (j[:, None] > j[None, :]).astype(np.float32)
    tmat = np.concatenate([suffix, np.ones((LANES, LANES), np.float32)], axis=1)
    incl = (j[:, None] <= j[None, :]).astype(np.float32)
    blk = np.arange(SB_W) // D_HEAD
    bd = (blk[:, None] == blk[None, :]).astype(np.float32)
    return jnp.asarray(tmat, BF16), jnp.asarray(incl, BF16), jnp.asarray(bd, BF16)


def _pad_rows(a, rows):
    return jnp.pad(a, ((0, 0), (0, rows - a.shape[1]), (0, 0)))


def kernel(x_prompt, x_sample, cache_sb_k, cache_sb_v, cache_dsa_k, cache_dsa_v, cache_idx_k, state_ret, page_table, c_prompt, c_sample, norm1_g, ada1_w, ada1_b, w_in, sb_qn_g, sb_kn_g, dsa_qn_g, dsa_kn_g, ret_gn_g, w_br_sb, w_br_ret, w_br_dsa, w_out, norm2_g, ada2_w, ada2_b, peer_wq, peer_subkeys, peer_u, peer_v):
    nb, s_len, d = x_prompt.shape
    ns, t_len, _ = x_sample.shape
    depth = w_in.shape[0]
    npool, page = cache_sb_k.shape[1], cache_sb_k.shape[2]
    npages = page_table.shape[1]
    past = npages * page
    tp = nb * s_len
    ts = ns * t_len
    topk_p = min(DSA_TOPK_MAX, s_len // 4)
    topk_s = min(DSA_TOPK_MAX, (past + t_len) // 4)
    tb_p = 256
    tb_peer = 512 if tp % 512 == 0 else 256
    peer_ch = 1024

    tmat, incl, bd = _tri_constants()
    cos_p, sin_p = _rope_tables(jnp.arange(s_len))
    cos_s, sin_s = _rope_tables(jnp.tile(past + jnp.arange(t_len), ns))

    c_all = jnp.concatenate([c_prompt, c_sample], axis=0)
    ada1 = _ada(c_all, ada1_w, ada1_b)
    ada2 = _ada(c_all, ada2_w, ada2_b)

    kc_sb = cache_sb_k.reshape(depth, npool, page, SB_W)
    vc_sb = cache_sb_v.reshape(depth, npool, page, SB_W)
    kc_dsa = cache_dsa_k.reshape(depth, npool, page, DSA_KV_W)
    vc_dsa = cache_dsa_v.reshape(depth, npool, page, DSA_KV_W)
    eye_h = jnp.eye(SB_HEADS, dtype=F32)

    xp = x_prompt.reshape(tp, d)
    xs = x_sample.reshape(ts, d)
    outs = {k: [] for k in ("p_sb_k", "p_sb_v", "p_dsa_k", "p_dsa_v", "p_idx_k", "p_ret",
                            "s_sb_k", "s_sb_v", "s_dsa_k", "s_dsa_v", "s_idx_k", "s_ret")}
    for l in range(depth):
        w_packed = _pack_w_in(w_in[l], d)
        tile8 = lambda g: jnp.tile(g, 8).reshape(1, 512)
        gains = (tile8(sb_qn_g[l]), tile8(sb_kn_g[l]), tile8(dsa_qn_g[l]), jnp.tile(dsa_kn_g[l], 2).reshape(1, 128))
        wsb, wret, wdsa, wout = _bf(w_br_sb[l]), _bf(w_br_ret[l]), _bf(w_br_dsa[l]), _bf(w_out[l])
        wq = _bf(peer_wq[l])
        u_bf = _bf(peer_u[l])
        vt_bf = _bf(peer_v[l]).T
        n1g = norm1_g[l].reshape(1, d)
        n2g = norm2_g[l].reshape(1, d)
        gn_g = ret_gn_g[l].reshape(1, RET_V_W)

        def split3(a):
            return a[:, :d], a[:, d:2 * d], a[:, 2 * d:]

        sh1p, sc1p, gt1p = split3(ada1[l, :nb])
        sh1s, sc1s, gt1s = split3(ada1[l, nb:])
        sh2p, sc2p, gt2p = split3(ada2[l, :nb])
        sh2s, sc2s, gt2s = split3(ada2[l, nb:])

        m = _inproj(xp, sh1p, sc1p, n1g, w_packed, bd, *gains, cos_p, sin_p, tb_p)
        o_sb = _sb_prompt(m["sbq_hm"], m["sbk_hm"], m["sbv_hm"], tmat, nb, s_len)
        o_ret, st_p = _retention(m["rq"].reshape(nb, s_len, -1), m["rk"].reshape(nb, s_len, -1),
                                 m["rv"].reshape(nb, s_len, -1), m["rg"].reshape(nb, s_len, -1),
                                 jnp.zeros((nb, RET_HEADS, RET_DK, RET_DV), F32), gn_g, RET_CHUNK)
        o_dsa = _dsa_prompt(m["iq"], m["iw"], m["ik"], m["dq_hm"], m["dk_hm"], m["dv_hm"], incl, nb, s_len, topk_p)
        xp = _merge(o_sb, o_ret.reshape(tp, -1), o_dsa, m["g"], xp, gt1p, wsb, wret, wdsa, wout, tb_p)
        h2t, th, e1, s2, e2 = _peer_scores(xp, sh2p, sc2p, n2g, wq, peer_subkeys[l], tb_p)
        xp = _peer_experts(h2t, u_bf, vt_bf, th, e1, s2, e2, xp, gt2p, tb_peer, peer_ch)
        outs["p_sb_k"].append(m["sbk"].reshape(nb, s_len, SB_HEADS, D_HEAD))
        outs["p_sb_v"].append(m["sbv"].reshape(nb, s_len, SB_HEADS, D_HEAD))
        outs["p_dsa_k"].append(m["dk"].reshape(nb, s_len, DSA_KV_HEADS, D_HEAD))
        outs["p_dsa_v"].append(m["dv"].reshape(nb, s_len, DSA_KV_HEADS, D_HEAD))
        outs["p_idx_k"].append(m["ik"].reshape(nb, s_len, IDX_DIM))
        outs["p_ret"].append(st_p)

        m = _inproj(xs, sh1s, sc1s, n1g, w_packed, bd, *gains, cos_s, sin_s, ts)
        q_sb = m["sbq_hm"].astype(F32).reshape(SB_HEADS, ns, t_len, D_HEAD)
        qbd = jnp.einsum("hntd,hg->ntghd", q_sb, eye_h).reshape(ns, t_len * SB_HEADS, SB_W)
        k_new = _pad_rows(m["sbk"].reshape(ns, t_len, SB_W), page)
        v_new = _pad_rows(m["sbv"].reshape(ns, t_len, SB_W), page)
        ik_new = _pad_rows(m["ik"].reshape(ns, t_len, IDX_DIM), page)
        iq16 = m["iq"].reshape(ns, t_len, IDX_HEADS, IDX_DIM).transpose(0, 2, 1, 3).reshape(ns, IDX_HEADS * t_len, IDX_DIM)
        iw16 = m["iw"].reshape(ns, t_len, IDX_HEADS).transpose(0, 2, 1).reshape(ns, IDX_HEADS * t_len, 1)
        o_sb_s, scores = _sample_sb_and_scores(page_table, _bf(qbd), kc_sb, vc_sb, k_new, v_new,
                                               cache_idx_k, ik_new, iq16, iw16, tmat, l)
        width = scores.shape[-1]
        mask = _sample_select(scores.reshape(ts, width), incl, topk_s, past, t_len).reshape(ns, t_len, width)
        mask8 = jnp.tile(mask, (1, 8 // t_len, 1))
        dq = m["dq_hm"].reshape(DSA_KV_HEADS, DSA_GROUP, ns, t_len, D_HEAD).transpose(2, 0, 1, 3, 4)
        dq = dq.reshape(ns, DSA_KV_HEADS, DSA_GROUP * t_len, D_HEAD)
        dk_new = _pad_rows(m["dk"].reshape(ns, t_len, DSA_KV_W), page)
        dv_new = _pad_rows(m["dv"].reshape(ns, t_len, DSA_KV_W), page)
        o_dsa_s = _sample_dsa(page_table, dq, kc_dsa, vc_dsa, dk_new, dv_new, mask8, l)
        pad = lambda a: _pad_rows(a.reshape(ns, t_len, -1), RET_CHUNK)
        o_ret_s, st_s = _retention(pad(m["rq"]), pad(m["rk"]), pad(m["rv"]), pad(m["rg"]), state_ret[l], gn_g, t_len)
        o_sb_hm = o_sb_s.reshape(ts, SB_HEADS, D_HEAD).transpose(1, 0, 2)
        o_dsa_hm = o_dsa_s.reshape(ns, DSA_KV_HEADS, DSA_GROUP, t_len, D_HEAD).transpose(1, 2, 0, 3, 4)
        o_dsa_hm = o_dsa_hm.reshape(DSA_HEADS, ts, D_HEAD)
        xs = _merge(o_sb_hm, o_ret_s[:, :t_len].reshape(ts, -1), o_dsa_hm, m["g"], xs, gt1s, wsb, wret, wdsa, wout, ts)
        h2t, th, e1, s2, e2 = _peer_scores(xs, sh2s, sc2s, n2g, wq, peer_subkeys[l], ts)
        xs = _peer_experts(h2t, u_bf, vt_bf, th, e1, s2, e2, xs, gt2s, ts, peer_ch)
        outs["s_sb_k"].append(m["sbk"].reshape(ns, t_len, SB_HEADS, D_HEAD))
        outs["s_sb_v"].append(m["sbv"].reshape(ns, t_len, SB_HEADS, D_HEAD))
        outs["s_dsa_k"].append(m["dk"].reshape(ns, t_len, DSA_KV_HEADS, D_HEAD))
        outs["s_dsa_v"].append(m["dv"].reshape(ns, t_len, DSA_KV_HEADS, D_HEAD))
        outs["s_idx_k"].append(m["ik"].reshape(ns, t_len, IDX_DIM))
        outs["s_ret"].append(st_s)

    st = {k: jnp.stack(v) for k, v in outs.items()}
    return (xp.reshape(nb, s_len, d), xs.reshape(ns, t_len, d),
            st["p_sb_k"], st["p_sb_v"], st["p_dsa_k"], st["p_dsa_v"], st["p_idx_k"], st["p_ret"],
            st["s_sb_k"], st["s_sb_v"], st["s_dsa_k"], st["s_dsa_v"], st["s_idx_k"], st["s_ret"])
```

```python
import functools

import numpy as np
import jax
import jax.numpy as jnp
from jax import lax
from jax.experimental import pallas as pl
from jax.experimental.pallas import tpu as pltpu

F32 = jnp.float32
BF16 = jnp.bfloat16

D_HEAD = 64
SB_HEADS = 8
SB_W = SB_HEADS * D_HEAD
RET_HEADS = 4
RET_DK = 64
RET_DV = 128
RET_QK_W = RET_HEADS * RET_DK
RET_V_W = RET_HEADS * RET_DV
RET_CHUNK = 128
DSA_HEADS = 8
DSA_KV_HEADS = 2
DSA_GROUP = DSA_HEADS // DSA_KV_HEADS
DSA_W = DSA_HEADS * D_HEAD
DSA_KV_W = DSA_KV_HEADS * D_HEAD
IDX_HEADS = 4
IDX_DIM = 64
IDX_W = IDX_HEADS * IDX_DIM
DSA_TOPK_MAX = 256
Q_BLOCK = 128
N_BRANCH = 3
PEER_HEADS = 8
PEER_NKEYS = 128
PEER_EXPERTS = PEER_NKEYS * PEER_NKEYS
PEER_DQ = 256
PEER_HALF = PEER_DQ // 2
PEER_TOPK = 16
RMS_EPS = 1e-6
GN_EPS = 1e-5
ROPE_BASE = 10000.0

LANES = 128
VMEM_LIMIT = 56 * 1024 * 1024
INT_MIN = np.int32(-2 ** 31)
NEG_INF = float("-inf")


def _bf(x):
    return x.astype(BF16)


def _split(x):
    hi = x.astype(BF16)
    lo = (x - hi.astype(F32)).astype(BF16)
    return hi, lo


def _dot(a, b):
    return jnp.dot(a, b, preferred_element_type=F32)


def _dot_nt(a, b):
    return lax.dot_general(a, b, (((1,), (1,)), ((), ())), preferred_element_type=F32)


def _dot_tn(a, b):
    return lax.dot_general(a, b, (((0,), (0,)), ((), ())), preferred_element_type=F32)


def _dot3_nt(a, b):
    ah, al = _split(a)
    bh, bl = _split(b)
    return _dot_nt(ah, bh) + _dot_nt(ah, bl) + _dot_nt(al, bh)


def _sigmoid(x):
    return 1.0 / (1.0 + jnp.exp(-x))


def _params(sem):
    return pltpu.CompilerParams(dimension_semantics=sem, vmem_limit_bytes=VMEM_LIMIT)


def _full_spec(shape):
    nd = len(shape)
    return pl.BlockSpec(shape, lambda *_: (0,) * nd, pipeline_mode=pl.Buffered(1))


def _mod_operand(mod, n_tok, tb):
    g, d = mod.shape
    tg = n_tok // g
    if tg % tb == 0:
        per = tg // tb
        return mod.reshape(g, 1, d), pl.BlockSpec((None, 1, d), lambda i, *_: (i // per, 0, 0))
    return jnp.repeat(mod, tg, axis=0), pl.BlockSpec((tb, d), lambda i, *_: (i, 0))


def _rms_mod(x, g, shift, scale):
    ms = jnp.mean(x * x, axis=-1, keepdims=True)
    return (x * lax.rsqrt(ms + RMS_EPS) * g) * (1.0 + scale) + shift


def _sortable_key(score):
    bits = lax.bitcast_convert_type(score, jnp.int32)
    return jnp.where(bits < 0, bits ^ np.int32(0x7FFFFFFF), bits)


def _kth_largest_key(key, k):
    rows = key.shape[0]

    def body(b, ans):
        cand = ans | jnp.left_shift(jnp.int32(1), 31 - b)
        cnt = jnp.sum(jnp.where(key >= (cand ^ INT_MIN), 1.0, 0.0), axis=1, keepdims=True)
        return jnp.where(cnt >= k, cand, ans)

    ans = lax.fori_loop(0, 32, body, jnp.zeros((rows, 1), jnp.int32))
    return ans ^ INT_MIN


def _topk_mask(score, k, incl):
    key = _sortable_key(score)
    thr = _kth_largest_key(key, k)
    gt = key > thr
    tie = key == thr
    need = k - jnp.sum(jnp.where(gt, 1.0, 0.0), axis=1, keepdims=True)
    out = []
    off = jnp.zeros_like(need)
    for c in range(score.shape[1] // LANES):
        sl = slice(c * LANES, (c + 1) * LANES)
        tie_c = tie[:, sl]
        rank = _dot(jnp.where(tie_c, 1.0, 0.0).astype(BF16), incl) + off
        out.append(jnp.where(gt[:, sl] | (tie_c & (rank <= need)), 1.0, 0.0))
        off = rank[:, LANES - 1:LANES]
    return jnp.concatenate(out, axis=1)


def _stick_tile(z, valid, tmat, tot):
    l1p = jnp.log1p(jnp.exp(-jnp.abs(z)))
    ls = jnp.minimum(z, 0.0) - l1p
    lk = -jnp.maximum(z, 0.0) - l1p
    if valid is not None:
        lk = jnp.where(valid, lk, 0.0)
    hi, lo = _split(lk)
    mm = _dot(hi, tmat) + _dot(lo, tmat)
    a = jnp.exp(ls + mm[:, :LANES] + tot)
    if valid is not None:
        a = jnp.where(valid, a, 0.0)
    return a, tot + mm[:, LANES:]


def _ada_kernel(c_ref, w_ref, b_ref, o_ref):
    c = c_ref[...]
    o_ref[...] = _dot(_bf(c * _sigmoid(c)), _bf(w_ref[...])) + b_ref[...]


def _ada(c, w, b):
    n, d = c.shape
    nl, _, w3 = w.shape
    tn = 1024
    return pl.pallas_call(
        _ada_kernel,
        grid=(nl, w3 // tn),
        in_specs=[pl.BlockSpec((n, d), lambda l, j: (0, 0)),
                  pl.BlockSpec((None, d, tn), lambda l, j: (l, 0, j)),
                  pl.BlockSpec((None, 1, tn), lambda l, j: (l, 0, j))],
        out_specs=pl.BlockSpec((None, n, tn), lambda l, j: (l, 0, j)),
        out_shape=jax.ShapeDtypeStruct((nl, n, w3), F32),
        compiler_params=_params(("arbitrary", "arbitrary")),
        name="ada",
    )(c, w, b.reshape(nl, 1, w3))


_C_SBQ, _C_SBK, _C_SBV = 0, 512, 1024
_C_RQ, _C_RK, _C_RV, _C_RG = 1536, 1792, 2048, 2560
_C_DQ, _C_DK, _C_DV, _C_IQ = 3072, 3584, 3712, 3840
_C_IK, _C_IW, _C_G = 4096, 4160, 4224
_W_PACKED = _C_G + N_BRANCH * 1024


def _pack_w_in(w, d_model):
    assert _W_PACKED == _C_G + N_BRANCH * d_model
    a = w[:, :4096]
    iw = w[:, 4096:4100]
    ik = w[:, 4100:4164]
    g = w[:, 4164:]
    pad = jnp.zeros((w.shape[0], _C_G - _C_IW - IDX_HEADS), w.dtype)
    return jnp.concatenate([a, ik, iw, pad, g], axis=1).astype(BF16)


def _head_rms(y, bd, g):
    hi, lo = _split(y * y)
    ms = (_dot(hi, bd) + _dot(lo, bd)) * (1.0 / D_HEAD)
    return y * lax.rsqrt(ms + RMS_EPS) * g


def _rotary(y, cos, sin, first_half):
    w = y.shape[1]
    partner = jnp.where(first_half, pltpu.roll(y, w - D_HEAD // 2, 1), pltpu.roll(y, D_HEAD // 2, 1))
    return y * cos + partner * sin


def _inproj_kernel(x_ref, sh_ref, sc_ref, ng_ref, w_ref, bd_ref, gq_ref, gk_ref, gdq_ref, gdk_ref,
                   cos_ref, sin_ref,
                   sbq_hm, sbk_o, sbv_o, sbk_hm, sbv_hm, rq_o, rk_o, rv_o, rg_o,
                   dq_hm, dk_o, dv_o, dk_hm, dv_hm, iq_o, ik_o, iw_o, g_o):
    h = _bf(_rms_mod(x_ref[...], ng_ref[...], sh_ref[...], sc_ref[...]))

    def proj(c0, width):
        return _dot(h, w_ref[:, c0:c0 + width])

    def to_heads(y, ref, nh):
        for i in range(nh):
            ref[i] = y[:, i * D_HEAD:(i + 1) * D_HEAD].astype(ref.dtype)

    bd = bd_ref[...]
    to_heads(_head_rms(proj(_C_SBQ, SB_W), bd, gq_ref[...]), sbq_hm, SB_HEADS)
    sbk = _head_rms(proj(_C_SBK, SB_W), bd, gk_ref[...])
    sbk_o[...] = sbk
    to_heads(sbk, sbk_hm, SB_HEADS)
    sbv = proj(_C_SBV, SB_W)
    sbv_o[...] = sbv
    to_heads(sbv, sbv_hm, SB_HEADS)

    cos = cos_ref[...]
    sin = sin_ref[...]
    lane = lax.broadcasted_iota(jnp.int32, cos.shape, 1)
    first_half = (lane % D_HEAD) < (D_HEAD // 2)
    rq_o[...] = _rotary(proj(_C_RQ, RET_QK_W), cos, sin, first_half)
    rk_o[...] = _rotary(proj(_C_RK, RET_QK_W), cos, sin, first_half) * (RET_DK ** -0.5)
    rv_o[...] = proj(_C_RV, RET_V_W)
    rg_o[...] = proj(_C_RG, RET_V_W)

    to_heads(_head_rms(proj(_C_DQ, DSA_W), bd, gdq_ref[...]), dq_hm, DSA_HEADS)
    dk = _head_rms(proj(_C_DK, DSA_KV_W), bd_ref[:DSA_KV_W, :DSA_KV_W], gdk_ref[...])
    dk_o[...] = dk
    to_heads(dk, dk_hm, DSA_KV_HEADS)
    dv = proj(_C_DV, DSA_KV_W)
    dv_o[...] = dv
    to_heads(dv, dv_hm, DSA_KV_HEADS)

    iq_o[...] = proj(_C_IQ, IDX_W)
    ikw = proj(_C_IK, LANES)
    ik_o[...] = ikw[:, :IDX_DIM]
    iw_o[...] = ikw[:, IDX_DIM:IDX_DIM + IDX_HEADS] * (IDX_HEADS ** -0.5)
    g_o[...] = _sigmoid(proj(_C_G, g_o.shape[1]))


def _inproj(x, shift, scale, norm_g, w_packed, bd, gq, gk, gdq, gdk, cos, sin, tb):
    t, d = x.shape
    nblk = t // tb
    rep = cos.shape[0] // tb
    sh_arr, sh_spec = _mod_operand(shift, t, tb)
    sc_arr, sc_spec = _mod_operand(scale, t, tb)

    def tok(width):
        return pl.BlockSpec((tb, width), lambda i: (i, 0))

    def hm(nh):
        return pl.BlockSpec((nh, tb, D_HEAD), lambda i: (0, i, 0))

    def sds(shape, dt=F32):
        return jax.ShapeDtypeStruct(shape, dt)

    out_specs = [hm(8), tok(512), tok(512), hm(8), hm(8), tok(256), tok(256), tok(512), tok(512),
                 hm(8), tok(128), tok(128), hm(2), hm(2), tok(256), tok(64), tok(4), tok(3 * d)]
    out_shape = [sds((8, t, 64), BF16), sds((t, 512)), sds((t, 512)), sds((8, t, 64), BF16), sds((8, t, 64), BF16),
                 sds((t, 256)), sds((t, 256)), sds((t, 512)), sds((t, 512)),
                 sds((8, t, 64), BF16), sds((t, 128)), sds((t, 128)), sds((2, t, 64), BF16), sds((2, t, 64), BF16),
                 sds((t, 256)), sds((t, 64)), sds((t, 4)), sds((t, 3 * d))]
    names = ["sbq_hm", "sbk", "sbv", "sbk_hm", "sbv_hm", "rq", "rk", "rv", "rg",
             "dq_hm", "dk", "dv", "dk_hm", "dv_hm", "iq", "ik", "iw", "g"]
    outs = pl.pallas_call(
        _inproj_kernel,
        grid=(nblk,),
        in_specs=[tok(d), sh_spec, sc_spec, _full_spec((1, d)), _full_spec(w_packed.shape), _full_spec(bd.shape),
                  _full_spec((1, 512)), _full_spec((1, 512)), _full_spec((1, 512)), _full_spec((1, 128)),
                  pl.BlockSpec((tb, 256), lambda i: (i % rep, 0)), pl.BlockSpec((tb, 256), lambda i: (i % rep, 0))],
        out_specs=out_specs,
        out_shape=out_shape,
        compiler_params=_params(("arbitrary",)),
        name="inproj",
    )(x, sh_arr, sc_arr, norm_g, w_packed, bd, gq, gk, gdq, gdk, cos, sin)
    return dict(zip(names, outs))


def _sbp_kernel(q_ref, k_ref, v_ref, t_ref, o_ref, acc_ref, tot_ref):
    i = pl.program_id(2)
    qb = q_ref.shape[0]
    nd = qb // LANES
    q = q_ref[...]
    tmat = t_ref[...]
    row = lax.broadcasted_iota(jnp.int32, (qb, LANES), 0)
    col = lax.broadcasted_iota(jnp.int32, (qb, LANES), 1)
    acc_ref[...] = jnp.zeros_like(acc_ref)
    tot_ref[...] = jnp.zeros_like(tot_ref)

    def tile(j, valid):
        off = pl.multiple_of(j * LANES, LANES)
        z = _dot_nt(q, k_ref[pl.ds(off, LANES), :]) * (D_HEAD ** -0.5)
        a, tot = _stick_tile(z, valid, tmat, tot_ref[...])
        tot_ref[...] = tot
        acc_ref[...] += _dot(_bf(a), v_ref[pl.ds(off, LANES), :])

    for d in reversed(range(nd)):
        tile(i * nd + d, col + d * LANES < row)

    def body(jj, carry):
        tile(i * nd - 1 - jj, None)
        return carry

    lax.fori_loop(0, i * nd, body, 0)
    o_ref[...] = acc_ref[...]


def _sb_prompt(q_hm, k_hm, v_hm, tmat, nb, s, qb):
    h, t, _ = q_hm.shape
    nq = s // qb
    return pl.pallas_call(
        _sbp_kernel,
        grid=(nb, h, nq),
        in_specs=[pl.BlockSpec((None, qb, D_HEAD), lambda b, hh, i: (hh, b * nq + i, 0)),
                  pl.BlockSpec((None, s, D_HEAD), lambda b, hh, i: (hh, b, 0)),
                  pl.BlockSpec((None, s, D_HEAD), lambda b, hh, i: (hh, b, 0)),
                  _full_spec(tmat.shape)],
        out_specs=pl.BlockSpec((None, qb, D_HEAD), lambda b, hh, i: (hh, b * nq + i, 0)),
        out_shape=jax.ShapeDtypeStruct((h, t, D_HEAD), F32),
        scratch_shapes=[pltpu.VMEM((qb, D_HEAD), F32), pltpu.VMEM((qb, LANES), F32)],
        compiler_params=_params(("arbitrary", "arbitrary", "arbitrary")),
        name="sb_prompt",
    )(q_hm, k_hm, v_hm, tmat)


def _ret_log_gamma():
    return np.log1p(-np.exp2(-5.0 - np.arange(RET_HEADS, dtype=np.float32))).astype(np.float32)


def _ret_kernel(q_ref, k_ref, v_ref, g_ref, s0_ref, dec_ref, qd_ref, kd_ref, gn_ref, o_ref, s1_ref, s_scr, *, sdecay):
    c = pl.program_id(1)

    @pl.when(c == 0)
    def _():
        s_scr[...] = s0_ref[...]

    q = q_ref[...]
    k = k_ref[...]
    v = v_ref[...]
    outs = []
    for h in range(RET_HEADS):
        qh = _bf(q[:, h * RET_DK:(h + 1) * RET_DK])
        kh = k[:, h * RET_DK:(h + 1) * RET_DK]
        vh = _bf(v[:, h * RET_DV:(h + 1) * RET_DV])
        s = s_scr[h]
        inner = _dot_nt(qh, _bf(kh)) * dec_ref[h]
        o = _dot(_bf(inner), vh) + _dot(qh, _bf(s)) * qd_ref[h]
        s_scr[h] = sdecay[h] * s + _dot_tn(_bf(kh * kd_ref[h]), vh)
        mu = jnp.mean(o, axis=-1, keepdims=True)
        var = jnp.mean(jnp.square(o - mu), axis=-1, keepdims=True)
        outs.append((o - mu) * lax.rsqrt(var + GN_EPS))
    on = jnp.concatenate(outs, axis=1) * gn_ref[...]
    rg = g_ref[...]
    o_ref[...] = (rg * _sigmoid(rg)) * on

    @pl.when(c == pl.num_programs(1) - 1)
    def _():
        s1_ref[...] = s_scr[...]


def _retention(q, k, v, rg, s0, gn_g, c_eff):
    n, s, _ = q.shape
    cc = RET_CHUNK
    lg = _ret_log_gamma().astype(np.float64)
    idx = np.arange(cc, dtype=np.float64)
    diff = idx[:, None] - idx[None, :]
    dec = np.where(diff[None] >= 0, np.exp(np.maximum(diff, 0.0)[None] * lg[:, None, None]), 0.0)
    qd = np.exp((idx + 1.0)[None, :, None] * lg[:, None, None])
    kd = np.exp((c_eff - 1.0 - idx)[None, :, None] * lg[:, None, None])
    kd = np.where(idx[None, :, None] < c_eff, kd, 0.0)
    sdecay = tuple(float(x) for x in np.exp(c_eff * lg))

    def tok(width):
        return pl.BlockSpec((None, cc, width), lambda i, c: (i, c, 0))

    st = pl.BlockSpec((None, RET_HEADS, RET_DK, RET_DV), lambda i, c: (i, 0, 0, 0))
    return pl.pallas_call(
        functools.partial(_ret_kernel, sdecay=sdecay),
        grid=(n, s // cc),
        in_specs=[tok(256), tok(256), tok(512), tok(512), st,
                  _full_spec((RET_HEADS, cc, cc)), _full_spec((RET_HEADS, cc, 1)), _full_spec((RET_HEADS, cc, 1)),
                  _full_spec((1, RET_V_W))],
        out_specs=[tok(512), st],
        out_shape=[jax.ShapeDtypeStruct((n, s, RET_V_W), F32),
                   jax.ShapeDtypeStruct((n, RET_HEADS, RET_DK, RET_DV), F32)],
        scratch_shapes=[pltpu.VMEM((RET_HEADS, RET_DK, RET_DV), F32)],
        compiler_params=_params(("arbitrary", "arbitrary")),
        name="retention",
    )(q, k, v, rg, s0, jnp.asarray(dec, F32), jnp.asarray(qd, F32), jnp.asarray(kd, F32), gn_g)


def _dsap_kernel(iq_ref, iw_ref, ik_ref, q_ref, k_ref, v_ref, incl_ref, o_ref, *, topk):
    i = pl.program_id(1)
    s = ik_ref.shape[0]
    iq = iq_ref[...]
    iw = iw_ref[...]
    ik = ik_ref[...]
    score = jnp.zeros((Q_BLOCK, s), F32)
    for h in range(IDX_HEADS):
        dots = _dot3_nt(iq[:, h * IDX_DIM:(h + 1) * IDX_DIM], ik) * (IDX_DIM ** -0.5)
        score = score + iw[:, h:h + 1] * jnp.maximum(dots, 0.0)
    qpos = i * Q_BLOCK + lax.broadcasted_iota(jnp.int32, (Q_BLOCK, s), 0)
    kpos = lax.broadcasted_iota(jnp.int32, (Q_BLOCK, s), 1)
    causal = kpos <= qpos
    score = jnp.where(causal, score + 0.0, NEG_INF)
    sel = (_topk_mask(score, topk, incl_ref[...]) > 0.5) & causal
    for g in range(DSA_KV_HEADS):
        kg = k_ref[g]
        vg = v_ref[g]
        for j in range(DSA_GROUP):
            hh = g * DSA_GROUP + j
            logits = jnp.where(sel, _dot_nt(q_ref[hh], kg) * (D_HEAD ** -0.5), NEG_INF)
            m = jnp.max(logits, axis=-1, keepdims=True)
            p = jnp.exp(logits - m)
            p = p / jnp.sum(p, axis=-1, keepdims=True)
            o_ref[hh] = _dot(_bf(p), vg)


def _dsa_prompt(iq, iw, ik, q_hm, k_hm, v_hm, incl, nb, s, topk):
    t = iq.shape[0]
    nq = s // Q_BLOCK
    return pl.pallas_call(
        functools.partial(_dsap_kernel, topk=topk),
        grid=(nb, nq),
        in_specs=[pl.BlockSpec((Q_BLOCK, IDX_W), lambda b, i: (b * nq + i, 0)),
                  pl.BlockSpec((Q_BLOCK, IDX_HEADS), lambda b, i: (b * nq + i, 0)),
                  pl.BlockSpec((s, IDX_DIM), lambda b, i: (b, 0)),
                  pl.BlockSpec((DSA_HEADS, Q_BLOCK, D_HEAD), lambda b, i: (0, b * nq + i, 0)),
                  pl.BlockSpec((DSA_KV_HEADS, s, D_HEAD), lambda b, i: (0, b, 0)),
                  pl.BlockSpec((DSA_KV_HEADS, s, D_HEAD), lambda b, i: (0, b, 0)),
                  _full_spec(incl.shape)],
        out_specs=pl.BlockSpec((DSA_HEADS, Q_BLOCK, D_HEAD), lambda b, i: (0, b * nq + i, 0)),
        out_shape=jax.ShapeDtypeStruct((DSA_HEADS, t, D_HEAD), F32),
        compiler_params=_params(("arbitrary", "arbitrary")),
        name="dsa_prompt",
    )(iq, iw, ik, q_hm, k_hm, v_hm, incl)


def _merge_kernel(osb_ref, oret_ref, odsa_ref, g_ref, x_ref, gt_ref, wsb_ref, wret_ref, wdsa_ref, wout_ref, o_ref):
    d = x_ref.shape[1]
    osb = _bf(jnp.concatenate([osb_ref[h] for h in range(SB_HEADS)], axis=1))
    odsa = _bf(jnp.concatenate([odsa_ref[h] for h in range(DSA_HEADS)], axis=1))
    merged = (g_ref[:, :d] * _dot(osb, wsb_ref[...])
              + g_ref[:, d:2 * d] * _dot(_bf(oret_ref[...]), wret_ref[...])
              + g_ref[:, 2 * d:] * _dot(odsa, wdsa_ref[...]))
    o_ref[...] = x_ref[...] + gt_ref[...] * _dot(_bf(merged), wout_ref[...])


def _merge(osb_hm, oret, odsa_hm, g, x, gate, wsb, wret, wdsa, wout, tb):
    t, d = x.shape
    gt_arr, gt_spec = _mod_operand(gate, t, tb)

    def tok(width):
        return pl.BlockSpec((tb, width), lambda i: (i, 0))

    hm = pl.BlockSpec((8, tb, D_HEAD), lambda i: (0, i, 0))
    return pl.pallas_call(
        _merge_kernel,
        grid=(t // tb,),
        in_specs=[hm, tok(512), hm, tok(3 * d), tok(d), gt_spec,
                  _full_spec(wsb.shape), _full_spec(wret.shape), _full_spec(wdsa.shape), _full_spec(wout.shape)],
        out_specs=tok(d),
        out_shape=jax.ShapeDtypeStruct((t, d), F32),
        compiler_params=_params(("arbitrary",)),
        name="merge",
    )(osb_hm, oret, odsa_hm, g, x, gt_arr, wsb, wret, wdsa, wout)


def _top_vals(s, n):
    vals = []
    for _ in range(n):
        m = jnp.max(s, axis=0, keepdims=True)
        vals.append(m)
        s = jnp.where(s == m, NEG_INF, s)
    return vals


def _peer1_kernel(x_ref, sh_ref, sc_ref, ng_ref, wq_ref, keys_ref, h2t_ref, th_ref, e1_ref, s2_ref, e2_ref, q_scr):
    h = pl.program_id(1)

    @pl.when(h == 0)
    def _():
        h2 = _rms_mod(x_ref[...], ng_ref[...], sh_ref[...], sc_ref[...])
        h2t_ref[...] = _bf(h2.T)
        q = _dot(_bf(h2), wq_ref[...])
        for c in range(2 * PEER_HEADS):
            q_scr[c] = q[:, c * PEER_HALF:(c + 1) * PEER_HALF]

    s1 = _dot3_nt(keys_ref[0], q_scr[2 * h])
    s2 = _dot3_nt(keys_ref[1], q_scr[2 * h + 1])
    t1 = _top_vals(s1, PEER_TOPK)
    t2 = jnp.concatenate(_top_vals(s2, PEER_TOPK), axis=0)
    cand = jnp.concatenate([a + t2 for a in t1], axis=0)
    best = _top_vals(cand, PEER_TOPK + 1)
    z = jnp.zeros_like(best[0])
    for b in best[:PEER_TOPK]:
        z = z + jnp.exp(b - best[0])
    thr = 0.5 * (best[PEER_TOPK - 1] + best[PEER_TOPK])
    th_ref[...] = thr - s1
    e1_ref[...] = jnp.exp(s1 - t1[0]) / z
    s2_ref[...] = s2
    e2_ref[...] = jnp.exp(s2 - t2[0:1])


def _peer_scores(x, shift, scale, norm_g, wq, keys, tb):
    t, d = x.shape
    sh_arr, sh_spec = _mod_operand(shift, t, tb)
    sc_arr, sc_spec = _mod_operand(scale, t, tb)
    per_head = pl.BlockSpec((None, PEER_NKEYS, tb), lambda i, h: (h, 0, i))
    sds = jax.ShapeDtypeStruct((PEER_HEADS, PEER_NKEYS, t), F32)
    return pl.pallas_call(
        _peer1_kernel,
        grid=(t // tb, PEER_HEADS),
        in_specs=[pl.BlockSpec((tb, d), lambda i, h: (i, 0)), sh_spec, sc_spec, _full_spec((1, d)),
                  _full_spec(wq.shape),
                  pl.BlockSpec((None, 2, PEER_NKEYS, PEER_HALF), lambda i, h: (h, 0, 0, 0))],
        out_specs=[pl.BlockSpec((d, tb), lambda i, h: (0, i)), per_head, per_head, per_head, per_head],
        out_shape=[jax.ShapeDtypeStruct((d, t), BF16), sds, sds, sds, sds],
        scratch_shapes=[pltpu.VMEM((2 * PEER_HEADS, tb, PEER_HALF), F32)],
        compiler_params=_params(("arbitrary", "arbitrary")),
        name="peer_scores",
    )(x, sh_arr, sc_arr, norm_g, wq, keys)


def _peer2_kernel(h2t_ref, u_ref, vt_ref, th_ref, e1_ref, s2_ref, e2_ref, x_ref, gt_ref, o_ref,
                  acc_ref, act_ref, c_ref):
    j = pl.program_id(1)
    tb = x_ref.shape[0]
    ch = u_ref.shape[0]

    @pl.when(j == 0)
    def _():
        acc_ref[...] = jnp.zeros_like(acc_ref)

    act_ref[...] = _dot(u_ref[...], h2t_ref[...])

    def body(r, carry):
        ro = pl.multiple_of(r * PEER_NKEYS, PEER_NKEYS)
        for cs in range(tb // LANES):
            sl = slice(cs * LANES, (cs + 1) * LANES)
            w = jnp.zeros((PEER_NKEYS, LANES), F32)
            for h in range(PEER_HEADS):
                sel = s2_ref[h, :, sl] >= th_ref[h, r, :, sl]
                w = w + jnp.where(sel, e2_ref[h, :, sl], 0.0) * e1_ref[h, r, :, sl]
            a = act_ref[pl.ds(ro, PEER_NKEYS), sl]
            gelu = 0.5 * a * (1.0 + lax.erf(a * (2.0 ** -0.5)))
            c_ref[pl.ds(ro, PEER_NKEYS), sl] = _bf(gelu * w)
        return carry

    lax.fori_loop(0, ch // PEER_NKEYS, body, 0)
    acc_ref[...] += _dot(vt_ref[...], c_ref[...])

    @pl.when(j == pl.num_programs(1) - 1)
    def _():
        o_ref[...] = x_ref[...] + gt_ref[...] * acc_ref[...].T


def _peer_experts(h2t, u, vt, th, e1, s2, e2, x, gate, tb, ch):
    t, d = x.shape
    ne = u.shape[0]
    rows = ch // PEER_NKEYS
    gt_arr, gt_spec = _mod_operand(gate, t, tb)
    by_row = pl.BlockSpec((PEER_HEADS, rows, 1, tb), lambda i, j: (0, j, 0, i))
    th = th.reshape(PEER_HEADS, PEER_NKEYS, 1, t)
    e1 = e1.reshape(PEER_HEADS, PEER_NKEYS, 1, t)
    by_tok = pl.BlockSpec((PEER_HEADS, PEER_NKEYS, tb), lambda i, j: (0, 0, i))
    return pl.pallas_call(
        _peer2_kernel,
        grid=(t // tb, ne // ch),
        in_specs=[pl.BlockSpec((d, tb), lambda i, j: (0, i)),
                  pl.BlockSpec((ch, d), lambda i, j: (j, 0)),
                  pl.BlockSpec((d, ch), lambda i, j: (0, j)),
                  by_row, by_row, by_tok, by_tok,
                  pl.BlockSpec((tb, d), lambda i, j: (i, 0)), gt_spec],
        out_specs=pl.BlockSpec((tb, d), lambda i, j: (i, 0)),
        out_shape=jax.ShapeDtypeStruct((t, d), F32),
        scratch_shapes=[pltpu.VMEM((d, tb), F32), pltpu.VMEM((ch, tb), F32), pltpu.VMEM((ch, tb), BF16)],
        compiler_params=_params(("arbitrary", "arbitrary")),
        name="peer_experts",
    )(h2t, u, vt, th, e1, s2, e2, x, gt_arr)


def _pages_per_step(npages, want):
    g = want
    while npages % g:
        g //= 2
    return g


def _dot3(a, b):
    ah, al = _split(a)
    bh, bl = _split(b)
    return _dot(ah, bh) + _dot(ah, bl) + _dot(al, bh)


def _idx_scores(iq, iw, ikt, nt):
    sc = jnp.maximum(_dot3(iq, ikt) * (IDX_DIM ** -0.5), 0.0) * iw
    score = sc[0:nt]
    for h in range(1, IDX_HEADS):
        score = score + sc[h * nt:(h + 1) * nt]
    return score


def _sscore_kernel(pt_ref, iq_ref, iw_ref, in_ref, *refs, g):
    page_refs, (past_ref, new_ref) = refs[:g], refs[g:]
    s = pl.program_id(1)
    nt = new_ref.shape[0]
    iq = iq_ref[...]
    iw = iw_ref[...]

    @pl.when(s == 0)
    def _():
        new_ref[...] = _idx_scores(iq, iw, in_ref[...], nt)

    @pl.when(s > 0)
    def _():
        for p in range(g):
            past_ref[:, p * LANES:(p + 1) * LANES] = _idx_scores(iq, iw, page_refs[p][...], nt)


def _sample_scores(page_table, iq16, iw16, ikt_cache, ikt_new, layer, g):
    n, npages = page_table.shape
    nt = iq16.shape[1] // IDX_HEADS

    def page_spec(p):
        return pl.BlockSpec((None, None, IDX_DIM, LANES),
                            lambda i, s, pt: (layer, pt[i, (jnp.maximum(s, 1) - 1) * g + p], 0, 0))

    def per_n(rows, width):
        return pl.BlockSpec((None, rows, width), lambda i, s, pt: (i, 0, 0))

    grid_spec = pltpu.PrefetchScalarGridSpec(
        num_scalar_prefetch=1,
        grid=(n, npages // g + 1),
        in_specs=[per_n(IDX_HEADS * nt, IDX_DIM), per_n(IDX_HEADS * nt, 1), per_n(IDX_DIM, LANES)]
                 + [page_spec(p) for p in range(g)],
        out_specs=[pl.BlockSpec((None, nt, g * LANES), lambda i, s, pt: (i, 0, jnp.maximum(s, 1) - 1)),
                   per_n(nt, LANES)],
    )
    return pl.pallas_call(
        functools.partial(_sscore_kernel, g=g),
        grid_spec=grid_spec,
        out_shape=[jax.ShapeDtypeStruct((n, nt, npages * LANES), F32), jax.ShapeDtypeStruct((n, nt, LANES), F32)],
        compiler_params=_params(("arbitrary", "arbitrary")),
        name="sample_scores",
    )(page_table, iq16, iw16, ikt_new, *([ikt_cache] * g))


def _sattn_kernel(pt_ref, qsb_ref, qd_ref, t_ref, kn_ref, vn_ref, dkn_ref, dvn_ref, mn_ref, mp_ref, *refs, g):
    k_refs, v_refs, dk_refs, dv_refs = refs[:g], refs[g:2 * g], refs[2 * g:3 * g], refs[3 * g:4 * g]
    osb_ref, od_ref, acc_ref, tot_ref, mx_ref, l_ref, dacc_ref = refs[4 * g:]
    s = pl.program_id(1)
    nrow = qsb_ref.shape[0]
    drow = qd_ref.shape[0]
    qsb = qsb_ref[...]
    qd = qd_ref[...]
    tmat = t_ref[...]

    @pl.when(s == 0)
    def _():
        acc_ref[...] = jnp.zeros_like(acc_ref)
        tot_ref[...] = jnp.zeros_like(tot_ref)
        mx_ref[...] = jnp.full_like(mx_ref, -1e30)
        l_ref[...] = jnp.zeros_like(l_ref)
        dacc_ref[...] = jnp.zeros_like(dacc_ref)

    def attend(pages, sel, valid):
        acc = acc_ref[...]
        tot = tot_ref[...]
        logits = []
        for kt, vt, dkt, dvt in pages:
            z = _dot(qsb, _bf(kt)) * (D_HEAD ** -0.5)
            a, tot = _stick_tile(z, valid, tmat, tot)
            acc = acc + _dot_nt(_bf(a), _bf(vt))
            logits.append(_dot(qd, _bf(dkt)) * (D_HEAD ** -0.5))
        acc_ref[...] = acc
        tot_ref[...] = tot
        logits = jnp.concatenate(logits, axis=1)
        m_old = mx_ref[...]
        m_new = jnp.maximum(m_old, jnp.max(jnp.where(sel, logits, -1e30), axis=-1, keepdims=True))
        p = jnp.where(sel, jnp.exp(logits - m_new), 0.0)
        alpha = jnp.exp(m_old - m_new)
        mx_ref[...] = m_new
        l_ref[...] = alpha * l_ref[...] + jnp.sum(p, axis=-1, keepdims=True)
        dacc = alpha * dacc_ref[...]
        for c, (_, _, _, dvt) in enumerate(pages):
            dacc = dacc + _dot_nt(_bf(p[:, c * LANES:(c + 1) * LANES]), _bf(dvt))
        dacc_ref[...] = dacc

    def tile_rows(m8):
        return jnp.concatenate([m8] * (drow // m8.shape[0]), axis=0) > 0.5

    @pl.when(s == 0)
    def _():
        trow = lax.broadcasted_iota(jnp.int32, (nrow, LANES), 0) // SB_HEADS
        col = lax.broadcasted_iota(jnp.int32, (nrow, LANES), 1)
        attend([(kn_ref[...], vn_ref[...], dkn_ref[...], dvn_ref[...])], tile_rows(mn_ref[...]), col < trow)

    @pl.when(s > 0)
    def _():
        mp = mp_ref[...]
        sel = tile_rows(jnp.concatenate([mp[:, (g - 1 - p) * LANES:(g - p) * LANES] for p in range(g)], axis=1))
        attend([(k_refs[p][...], v_refs[p][...], dk_refs[p][...], dv_refs[p][...]) for p in range(g)], sel, None)

    @pl.when(s == pl.num_programs(1) - 1)
    def _():
        acc = acc_ref[...]
        hrow = lax.broadcasted_iota(jnp.int32, acc.shape, 0) % SB_HEADS
        hcol = lax.broadcasted_iota(jnp.int32, acc.shape, 1) // D_HEAD
        own = jnp.where(hrow == hcol, acc, 0.0)
        osb_ref[...] = jnp.sum(own.reshape(nrow // SB_HEADS, SB_HEADS, acc.shape[1]), axis=1)
        dacc = dacc_ref[...] / l_ref[...]
        half = drow // DSA_KV_HEADS
        grow = lax.broadcasted_iota(jnp.int32, dacc.shape, 0) // half
        gcol = lax.broadcasted_iota(jnp.int32, dacc.shape, 1) // D_HEAD
        down = jnp.where(grow == gcol, dacc, 0.0)
        od_ref[...] = down[:half] + down[half:]


def _sample_attention(page_table, qsb, qd, tmat, kt_new, vt_new, dkt_new, dvt_new, mask_new, mask_past,
                      kt_cache, vt_cache, dkt_cache, dvt_cache, layer, g):
    n, npages = page_table.shape
    nrow = qsb.shape[1]
    drow = qd.shape[1]
    nsteps = npages // g

    def page_spec(width, p):
        return pl.BlockSpec((None, None, width, LANES),
                            lambda i, s, pt: (layer, pt[i, npages - 1 - (jnp.maximum(s, 1) - 1) * g - p], 0, 0))

    def per_n(rows, width):
        return pl.BlockSpec((None, rows, width), lambda i, s, pt: (i, 0, 0))

    in_specs = ([per_n(nrow, SB_W), per_n(drow, DSA_KV_W), pl.BlockSpec(tmat.shape, lambda i, s, pt: (0, 0)),
                 per_n(SB_W, LANES), per_n(SB_W, LANES), per_n(DSA_KV_W, LANES), per_n(DSA_KV_W, LANES),
                 per_n(8, LANES),
                 pl.BlockSpec((None, 8, g * LANES), lambda i, s, pt: (i, 0, nsteps - jnp.maximum(s, 1)))]
                + [page_spec(SB_W, p) for p in range(g)] + [page_spec(SB_W, p) for p in range(g)]
                + [page_spec(DSA_KV_W, p) for p in range(g)] + [page_spec(DSA_KV_W, p) for p in range(g)])
    grid_spec = pltpu.PrefetchScalarGridSpec(
        num_scalar_prefetch=1,
        grid=(n, nsteps + 1),
        in_specs=in_specs,
        out_specs=[per_n(nrow // SB_HEADS, SB_W), per_n(drow // DSA_KV_HEADS, DSA_KV_W)],
        scratch_shapes=[pltpu.VMEM((nrow, SB_W), F32), pltpu.VMEM((nrow, LANES), F32),
                        pltpu.VMEM((drow, 1), F32), pltpu.VMEM((drow, 1), F32), pltpu.VMEM((drow, DSA_KV_W), F32)],
    )
    return pl.pallas_call(
        functools.partial(_sattn_kernel, g=g),
        grid_spec=grid_spec,
        out_shape=[jax.ShapeDtypeStruct((n, nrow // SB_HEADS, SB_W), F32),
                   jax.ShapeDtypeStruct((n, drow // DSA_KV_HEADS, DSA_KV_W), F32)],
        compiler_params=_params(("arbitrary", "arbitrary")),
        name="sample_attention",
    )(page_table, qsb, qd, tmat, kt_new, vt_new, dkt_new, dvt_new, mask_new, mask_past,
      *([kt_cache] * g), *([vt_cache] * g), *([dkt_cache] * g), *([dvt_cache] * g))


def _sthr_kernel(sc_ref, incl_ref, m_ref, *, topk, past, nt):
    rows, width = sc_ref.shape
    t = (pl.program_id(0) * rows + lax.broadcasted_iota(jnp.int32, (rows, width), 0)) % nt
    pos = lax.broadcasted_iota(jnp.int32, (rows, width), 1)
    causal = pos <= past + t
    score = jnp.where(causal, sc_ref[...] + 0.0, NEG_INF)
    m_ref[...] = jnp.where(causal, _topk_mask(score, topk, incl_ref[...]), 0.0)


def _sample_select(scores, incl, topk, past, nt):
    r, width = scores.shape
    rb = 8
    return pl.pallas_call(
        functools.partial(_sthr_kernel, topk=topk, past=past, nt=nt),
        grid=(r // rb,),
        in_specs=[pl.BlockSpec((rb, width), lambda i: (i, 0)), _full_spec(incl.shape)],
        out_specs=pl.BlockSpec((rb, width), lambda i: (i, 0)),
        out_shape=jax.ShapeDtypeStruct((r, width), F32),
        compiler_params=_params(("arbitrary",)),
        name="sample_select",
    )(scores, incl)


def _rope_tables(pos):
    half = D_HEAD // 2
    freqs = ROPE_BASE ** (-jnp.arange(half, dtype=F32) / half)
    ang = pos.astype(F32)[:, None] * freqs[None, :]
    cos = jnp.cos(ang)
    sin = jnp.sin(ang)
    return (jnp.tile(jnp.concatenate([cos, cos], axis=1), (1, RET_HEADS)),
            jnp.tile(jnp.concatenate([-sin, sin], axis=1), (1, RET_HEADS)))


def _tri_constants():
    j = np.arange(LANES)
    suffix = (j[:, None] > j[None, :]).astype(np.float32)
    tmat = np.concatenate([suffix, np.ones((LANES, LANES), np.float32)], axis=1)
    incl = (j[:, None] <= j[None, :]).astype(np.float32)
    blk = np.arange(SB_W) // D_HEAD
    bd = (blk[:, None] == blk[None, :]).astype(np.float32)
    return jnp.asarray(tmat, BF16), jnp.asarray(incl, BF16), jnp.asarray(bd, BF16)


def _pad_rows(a, rows):
    return jnp.pad(a, ((0, 0), (0, rows - a.shape[1]), (0, 0)))


def kernel(x_prompt, x_sample, cache_sb_k, cache_sb_v, cache_dsa_k, cache_dsa_v, cache_idx_k, state_ret, page_table, c_prompt, c_sample, norm1_g, ada1_w, ada1_b, w_in, sb_qn_g, sb_kn_g, dsa_qn_g, dsa_kn_g, ret_gn_g, w_br_sb, w_br_ret, w_br_dsa, w_out, norm2_g, ada2_w, ada2_b, peer_wq, peer_subkeys, peer_u, peer_v):
    nb, s_len, d = x_prompt.shape
    ns, t_len, _ = x_sample.shape
    depth = w_in.shape[0]
    npool, page = cache_sb_k.shape[1], cache_sb_k.shape[2]
    npages = page_table.shape[1]
    past = npages * page
    tp = nb * s_len
    ts = ns * t_len
    topk_p = min(DSA_TOPK_MAX, s_len // 4)
    topk_s = min(DSA_TOPK_MAX, (past + t_len) // 4)
    tb_p = 256
    tb_peer = 512 if tp % 512 == 0 else 256
    peer_ch = 1024
    sb_qb = 512 if s_len % 512 == 0 else Q_BLOCK

    tmat, incl, bd = _tri_constants()
    cos_p, sin_p = _rope_tables(jnp.arange(s_len))
    cos_s, sin_s = _rope_tables(jnp.tile(past + jnp.arange(t_len), ns))

    c_all = jnp.concatenate([c_prompt, c_sample], axis=0)
    ada1 = _ada(c_all, ada1_w, ada1_b)
    ada2 = _ada(c_all, ada2_w, ada2_b)

    assert page == LANES and 8 % t_len == 0
    kt_sb = cache_sb_k.transpose(0, 1, 3, 4, 2).reshape(depth, npool, SB_W, page)
    vt_sb = cache_sb_v.transpose(0, 1, 3, 4, 2).reshape(depth, npool, SB_W, page)
    kt_dsa = cache_dsa_k.transpose(0, 1, 3, 4, 2).reshape(depth, npool, DSA_KV_W, page)
    vt_dsa = cache_dsa_v.transpose(0, 1, 3, 4, 2).reshape(depth, npool, DSA_KV_W, page)
    ikt_cache = cache_idx_k.transpose(0, 1, 3, 2)
    eye_h = jnp.eye(SB_HEADS, dtype=F32)
    eye_g = jnp.eye(DSA_KV_HEADS, dtype=F32)
    g_attn = _pages_per_step(npages, 8)
    g_score = _pages_per_step(npages, 16)

    def new_keys_t(a, width):
        a = a.reshape(ns, t_len, width).transpose(0, 2, 1)
        return jnp.pad(a, ((0, 0), (0, 0), (0, page - t_len)))

    xp = x_prompt.reshape(tp, d)
    xs = x_sample.reshape(ts, d)
    outs = {k: [] for k in ("p_sb_k", "p_sb_v", "p_dsa_k", "p_dsa_v", "p_idx_k", "p_ret",
                            "s_sb_k", "s_sb_v", "s_dsa_k", "s_dsa_v", "s_idx_k", "s_ret")}
    for l in range(depth):
        w_packed = _pack_w_in(w_in[l], d)
        tile8 = lambda g: jnp.tile(g, 8).reshape(1, 512)
        gains = (tile8(sb_qn_g[l]), tile8(sb_kn_g[l]), tile8(dsa_qn_g[l]), jnp.tile(dsa_kn_g[l], 2).reshape(1, 128))
        wsb, wret, wdsa, wout = _bf(w_br_sb[l]), _bf(w_br_ret[l]), _bf(w_br_dsa[l]), _bf(w_out[l])
        wq = _bf(peer_wq[l])
        u_bf = _bf(peer_u[l])
        vt_bf = _bf(peer_v[l]).T
        n1g = norm1_g[l].reshape(1, d)
        n2g = norm2_g[l].reshape(1, d)
        gn_g = ret_gn_g[l].reshape(1, RET_V_W)

        def split3(a):
            return a[:, :d], a[:, d:2 * d], a[:, 2 * d:]

        sh1p, sc1p, gt1p = split3(ada1[l, :nb])
        sh1s, sc1s, gt1s = split3(ada1[l, nb:])
        sh2p, sc2p, gt2p = split3(ada2[l, :nb])
        sh2s, sc2s, gt2s = split3(ada2[l, nb:])

        m = _inproj(xp, sh1p, sc1p, n1g, w_packed, bd, *gains, cos_p, sin_p, tb_p)
        o_sb = _sb_prompt(m["sbq_hm"], m["sbk_hm"], m["sbv_hm"], tmat, nb, s_len, sb_qb)
        o_ret, st_p = _retention(m["rq"].reshape(nb, s_len, -1), m["rk"].reshape(nb, s_len, -1),
                                 m["rv"].reshape(nb, s_len, -1), m["rg"].reshape(nb, s_len, -1),
                                 jnp.zeros((nb, RET_HEADS, RET_DK, RET_DV), F32), gn_g, RET_CHUNK)
        o_dsa = _dsa_prompt(m["iq"], m["iw"], m["ik"], m["dq_hm"], m["dk_hm"], m["dv_hm"], incl, nb, s_len, topk_p)
        xp = _merge(o_sb, o_ret.reshape(tp, -1), o_dsa, m["g"], xp, gt1p, wsb, wret, wdsa, wout, tb_p)
        h2t, th, e1, s2, e2 = _peer_scores(xp, sh2p, sc2p, n2g, wq, peer_subkeys[l], tb_p)
        xp = _peer_experts(h2t, u_bf, vt_bf, th, e1, s2, e2, xp, gt2p, tb_peer, peer_ch)
        outs["p_sb_k"].append(m["sbk"].reshape(nb, s_len, SB_HEADS, D_HEAD))
        outs["p_sb_v"].append(m["sbv"].reshape(nb, s_len, SB_HEADS, D_HEAD))
        outs["p_dsa_k"].append(m["dk"].reshape(nb, s_len, DSA_KV_HEADS, D_HEAD))
        outs["p_dsa_v"].append(m["dv"].reshape(nb, s_len, DSA_KV_HEADS, D_HEAD))
        outs["p_idx_k"].append(m["ik"].reshape(nb, s_len, IDX_DIM))
        outs["p_ret"].append(st_p)

        m = _inproj(xs, sh1s, sc1s, n1g, w_packed, bd, *gains, cos_s, sin_s, ts)
        q_sb = m["sbq_hm"].astype(F32).reshape(SB_HEADS, ns, t_len, D_HEAD)
        qbd = jnp.einsum("hntd,hg->ntghd", q_sb, eye_h).reshape(ns, t_len * SB_HEADS, SB_W)
        iq16 = m["iq"].reshape(ns, t_len, IDX_HEADS, IDX_DIM).transpose(0, 2, 1, 3).reshape(ns, IDX_HEADS * t_len, IDX_DIM)
        iw16 = m["iw"].reshape(ns, t_len, IDX_HEADS).transpose(0, 2, 1).reshape(ns, IDX_HEADS * t_len, 1)
        sc_past, sc_new = _sample_scores(page_table, iq16, iw16, ikt_cache, new_keys_t(m["ik"], IDX_DIM), l, g_score)
        scores = jnp.concatenate([sc_past, sc_new], axis=-1)
        width = scores.shape[-1]
        mask = _sample_select(scores.reshape(ts, width), incl, topk_s, past, t_len).reshape(ns, t_len, width)
        mask8 = jnp.tile(mask, (1, 8 // t_len, 1))
        dq = m["dq_hm"].astype(F32).reshape(DSA_KV_HEADS, DSA_GROUP, ns, t_len, D_HEAD)
        qd = jnp.einsum("gjntd,gk->ngjtkd", dq, eye_g).reshape(ns, DSA_HEADS * t_len, DSA_KV_W)
        o_sb_s, o_dsa_s = _sample_attention(
            page_table, _bf(qbd), _bf(qd), tmat,
            new_keys_t(m["sbk"], SB_W), new_keys_t(m["sbv"], SB_W),
            new_keys_t(m["dk"], DSA_KV_W), new_keys_t(m["dv"], DSA_KV_W),
            mask8[:, :, past:], mask8[:, :, :past], kt_sb, vt_sb, kt_dsa, vt_dsa, l, g_attn)
        pad = lambda a: _pad_rows(a.reshape(ns, t_len, -1), RET_CHUNK)
        o_ret_s, st_s = _retention(pad(m["rq"]), pad(m["rk"]), pad(m["rv"]), pad(m["rg"]), state_ret[l], gn_g, t_len)
        o_sb_hm = o_sb_s.reshape(ts, SB_HEADS, D_HEAD).transpose(1, 0, 2)
        o_dsa_hm = o_dsa_s.reshape(ns, DSA_GROUP, t_len, DSA_KV_HEADS, D_HEAD).transpose(3, 1, 0, 2, 4)
        o_dsa_hm = o_dsa_hm.reshape(DSA_HEADS, ts, D_HEAD)
        xs = _merge(o_sb_hm, o_ret_s[:, :t_len].reshape(ts, -1), o_dsa_hm, m["g"], xs, gt1s, wsb, wret, wdsa, wout, ts)
        h2t, th, e1, s2, e2 = _peer_scores(xs, sh2s, sc2s, n2g, wq, peer_subkeys[l], ts)
        xs = _peer_experts(h2t, u_bf, vt_bf, th, e1, s2, e2, xs, gt2s, ts, peer_ch)
        outs["s_sb_k"].append(m["sbk"].reshape(ns, t_len, SB_HEADS, D_HEAD))
        outs["s_sb_v"].append(m["sbv"].reshape(ns, t_len, SB_HEADS, D_HEAD))
        outs["s_dsa_k"].append(m["dk"].reshape(ns, t_len, DSA_KV_HEADS, D_HEAD))
        outs["s_dsa_v"].append(m["dv"].reshape(ns, t_len, DSA_KV_HEADS, D_HEAD))
        outs["s_idx_k"].append(m["ik"].reshape(ns, t_len, IDX_DIM))
        outs["s_ret"].append(st_s)

    st = {k: jnp.stack(v) for k, v in outs.items()}
    return (xp.reshape(nb, s_len, d), xs.reshape(ns, t_len, d),
            st["p_sb_k"], st["p_sb_v"], st["p_dsa_k"], st["p_dsa_v"], st["p_idx_k"], st["p_ret"],
            st["s_sb_k"], st["s_sb_v"], st["s_dsa_k"], st["s_dsa_v"], st["s_idx_k"], st["s_ret"])
```

```python
import functools

import numpy as np
import jax
import jax.numpy as jnp
from jax import lax
from jax.experimental import pallas as pl
from jax.experimental.pallas import tpu as pltpu

F32 = jnp.float32
BF16 = jnp.bfloat16

D_HEAD = 64
SB_HEADS = 8
SB_W = SB_HEADS * D_HEAD
RET_HEADS = 4
RET_DK = 64
RET_DV = 128
RET_QK_W = RET_HEADS * RET_DK
RET_V_W = RET_HEADS * RET_DV
RET_CHUNK = 128
DSA_HEADS = 8
DSA_KV_HEADS = 2
DSA_GROUP = DSA_HEADS // DSA_KV_HEADS
DSA_W = DSA_HEADS * D_HEAD
DSA_KV_W = DSA_KV_HEADS * D_HEAD
IDX_HEADS = 4
IDX_DIM = 64
IDX_W = IDX_HEADS * IDX_DIM
DSA_TOPK_MAX = 256
Q_BLOCK = 128
N_BRANCH = 3
PEER_HEADS = 8
PEER_NKEYS = 128
PEER_EXPERTS = PEER_NKEYS * PEER_NKEYS
PEER_DQ = 256
PEER_HALF = PEER_DQ // 2
PEER_TOPK = 16
RMS_EPS = 1e-6
GN_EPS = 1e-5
ROPE_BASE = 10000.0

LANES = 128
VMEM_LIMIT = 56 * 1024 * 1024
INT_MIN = np.int32(-2 ** 31)
NEG_INF = float("-inf")


def _bf(x):
    return x.astype(BF16)


def _split(x):
    hi = x.astype(BF16)
    lo = (x - hi.astype(F32)).astype(BF16)
    return hi, lo


def _dot(a, b):
    return jnp.dot(a, b, preferred_element_type=F32)


def _dot_nt(a, b):
    return lax.dot_general(a, b, (((1,), (1,)), ((), ())), preferred_element_type=F32)


def _dot_tn(a, b):
    return lax.dot_general(a, b, (((0,), (0,)), ((), ())), preferred_element_type=F32)


def _dot3_nt(a, b):
    ah, al = _split(a)
    bh, bl = _split(b)
    return _dot_nt(ah, bh) + _dot_nt(ah, bl) + _dot_nt(al, bh)


def _sigmoid(x):
    return 1.0 / (1.0 + jnp.exp(-x))


def _params(sem):
    return pltpu.CompilerParams(dimension_semantics=sem, vmem_limit_bytes=VMEM_LIMIT)


def _full_spec(shape):
    nd = len(shape)
    return pl.BlockSpec(shape, lambda *_: (0,) * nd, pipeline_mode=pl.Buffered(1))


def _mod_operand(mod, n_tok, tb):
    g, d = mod.shape
    tg = n_tok // g
    if tg % tb == 0:
        per = tg // tb
        return mod.reshape(g, 1, d), pl.BlockSpec((None, 1, d), lambda i, *_: (i // per, 0, 0))
    return jnp.repeat(mod, tg, axis=0), pl.BlockSpec((tb, d), lambda i, *_: (i, 0))


def _rms_mod(x, g, shift, scale):
    ms = jnp.mean(x * x, axis=-1, keepdims=True)
    return (x * lax.rsqrt(ms + RMS_EPS) * g) * (1.0 + scale) + shift


def _sortable_key(score):
    bits = lax.bitcast_convert_type(score, jnp.int32)
    return jnp.where(bits < 0, bits ^ np.int32(0x7FFFFFFF), bits)


def _kth_largest_key(key, k):
    rows = key.shape[0]

    def body(b, ans):
        cand = ans | jnp.left_shift(jnp.int32(1), 31 - b)
        cnt = jnp.sum(jnp.where(key >= (cand ^ INT_MIN), 1.0, 0.0), axis=1, keepdims=True)
        return jnp.where(cnt >= k, cand, ans)

    ans = lax.fori_loop(0, 32, body, jnp.zeros((rows, 1), jnp.int32))
    return ans ^ INT_MIN


def _topk_mask(score, k, incl):
    key = _sortable_key(score)
    thr = _kth_largest_key(key, k)
    gt = key > thr
    tie = key == thr
    need = k - jnp.sum(jnp.where(gt, 1.0, 0.0), axis=1, keepdims=True)
    out = []
    off = jnp.zeros_like(need)
    for c in range(score.shape[1] // LANES):
        sl = slice(c * LANES, (c + 1) * LANES)
        tie_c = tie[:, sl]
        rank = _dot(jnp.where(tie_c, 1.0, 0.0).astype(BF16), incl) + off
        out.append(jnp.where(gt[:, sl] | (tie_c & (rank <= need)), 1.0, 0.0))
        off = rank[:, LANES - 1:LANES]
    return jnp.concatenate(out, axis=1)


def _stick_pre(z, valid, tmat):
    l1p = jnp.log1p(jnp.exp(-jnp.abs(z)))
    ls = jnp.minimum(z, 0.0) - l1p
    lk = -jnp.maximum(z, 0.0) - l1p
    if valid is not None:
        lk = jnp.where(valid, lk, 0.0)
    hi, lo = _split(lk)
    return ls, _dot(hi, tmat) + _dot(lo, tmat)


def _stick_weights(pre, valids, tot):
    out = []
    for (ls, mm), valid in zip(pre, valids):
        a = jnp.exp(ls + mm[:, :LANES] + tot)
        if valid is not None:
            a = jnp.where(valid, a, 0.0)
        out.append(a)
        tot = tot + mm[:, LANES:]
    return out, tot


def _ada_kernel(c_ref, w_ref, b_ref, o_ref):
    c = c_ref[...]
    o_ref[...] = _dot(_bf(c * _sigmoid(c)), _bf(w_ref[...])) + b_ref[...]


def _ada(c, w, b):
    n, d = c.shape
    nl, _, w3 = w.shape
    tn = 1024
    return pl.pallas_call(
        _ada_kernel,
        grid=(nl, w3 // tn),
        in_specs=[pl.BlockSpec((n, d), lambda l, j: (0, 0)),
                  pl.BlockSpec((None, d, tn), lambda l, j: (l, 0, j)),
                  pl.BlockSpec((None, 1, tn), lambda l, j: (l, 0, j))],
        out_specs=pl.BlockSpec((None, n, tn), lambda l, j: (l, 0, j)),
        out_shape=jax.ShapeDtypeStruct((nl, n, w3), F32),
        compiler_params=_params(("arbitrary", "arbitrary")),
        name="ada",
    )(c, w, b.reshape(nl, 1, w3))


_C_SBQ, _C_SBK, _C_SBV = 0, 512, 1024
_C_RQ, _C_RK, _C_RV, _C_RG = 1536, 1792, 2048, 2560
_C_DQ, _C_DK, _C_DV, _C_IQ = 3072, 3584, 3712, 3840
_C_IK, _C_IW, _C_G = 4096, 4160, 4224
_W_PACKED = _C_G + N_BRANCH * 1024


def _pack_w_in(w, d_model):
    assert _W_PACKED == _C_G + N_BRANCH * d_model
    a = w[:, :4096]
    iw = w[:, 4096:4100]
    ik = w[:, 4100:4164]
    g = w[:, 4164:]
    pad = jnp.zeros((w.shape[0], _C_G - _C_IW - IDX_HEADS), w.dtype)
    return jnp.concatenate([a, ik, iw, pad, g], axis=1).astype(BF16)


def _head_rms(y, bd, g):
    hi, lo = _split(y * y)
    ms = (_dot(hi, bd) + _dot(lo, bd)) * (1.0 / D_HEAD)
    return y * lax.rsqrt(ms + RMS_EPS) * g


def _rotary(y, cos, sin, first_half):
    w = y.shape[1]
    partner = jnp.where(first_half, pltpu.roll(y, w - D_HEAD // 2, 1), pltpu.roll(y, D_HEAD // 2, 1))
    return y * cos + partner * sin


def _inproj_kernel(x_ref, sh_ref, sc_ref, ng_ref, w_ref, bd_ref, gq_ref, gk_ref, gdq_ref, gdk_ref,
                   cos_ref, sin_ref,
                   sbq_hm, sbk_o, sbv_o, sbk_hm, sbv_hm, rq_o, rk_o, rv_o, rg_o,
                   dq_hm, dk_o, dv_o, dk_hm, dv_hm, iq_o, ik_o, iw_o, g_o):
    h = _bf(_rms_mod(x_ref[...], ng_ref[...], sh_ref[...], sc_ref[...]))

    def proj(c0, width):
        return _dot(h, w_ref[:, c0:c0 + width])

    def to_heads(y, ref, nh):
        for i in range(nh):
            ref[i] = y[:, i * D_HEAD:(i + 1) * D_HEAD].astype(ref.dtype)

    bd = bd_ref[...]
    to_heads(_head_rms(proj(_C_SBQ, SB_W), bd, gq_ref[...]), sbq_hm, SB_HEADS)
    sbk = _head_rms(proj(_C_SBK, SB_W), bd, gk_ref[...])
    sbk_o[...] = sbk
    to_heads(sbk, sbk_hm, SB_HEADS)
    sbv = proj(_C_SBV, SB_W)
    sbv_o[...] = sbv
    to_heads(sbv, sbv_hm, SB_HEADS)

    cos = cos_ref[...]
    sin = sin_ref[...]
    lane = lax.broadcasted_iota(jnp.int32, cos.shape, 1)
    first_half = (lane % D_HEAD) < (D_HEAD // 2)
    rq_o[...] = _rotary(proj(_C_RQ, RET_QK_W), cos, sin, first_half)
    rk_o[...] = _rotary(proj(_C_RK, RET_QK_W), cos, sin, first_half) * (RET_DK ** -0.5)
    rv_o[...] = proj(_C_RV, RET_V_W)
    rg_o[...] = proj(_C_RG, RET_V_W)

    to_heads(_head_rms(proj(_C_DQ, DSA_W), bd, gdq_ref[...]), dq_hm, DSA_HEADS)
    dk = _head_rms(proj(_C_DK, DSA_KV_W), bd_ref[:DSA_KV_W, :DSA_KV_W], gdk_ref[...])
    dk_o[...] = dk
    to_heads(dk, dk_hm, DSA_KV_HEADS)
    dv = proj(_C_DV, DSA_KV_W)
    dv_o[...] = dv
    to_heads(dv, dv_hm, DSA_KV_HEADS)

    iq_o[...] = proj(_C_IQ, IDX_W)
    ikw = proj(_C_IK, LANES)
    ik_o[...] = ikw[:, :IDX_DIM]
    iw_o[...] = ikw[:, IDX_DIM:IDX_DIM + IDX_HEADS] * (IDX_HEADS ** -0.5)
    g_o[...] = _sigmoid(proj(_C_G, g_o.shape[1]))


def _inproj(x, shift, scale, norm_g, w_packed, bd, gq, gk, gdq, gdk, cos, sin, tb):
    t, d = x.shape
    nblk = t // tb
    rep = cos.shape[0] // tb
    sh_arr, sh_spec = _mod_operand(shift, t, tb)
    sc_arr, sc_spec = _mod_operand(scale, t, tb)

    def tok(width):
        return pl.BlockSpec((tb, width), lambda i: (i, 0))

    def hm(nh):
        return pl.BlockSpec((nh, tb, D_HEAD), lambda i: (0, i, 0))

    def sds(shape, dt=F32):
        return jax.ShapeDtypeStruct(shape, dt)

    out_specs = [hm(8), tok(512), tok(512), hm(8), hm(8), tok(256), tok(256), tok(512), tok(512),
                 hm(8), tok(128), tok(128), hm(2), hm(2), tok(256), tok(64), tok(4), tok(3 * d)]
    out_shape = [sds((8, t, 64), BF16), sds((t, 512)), sds((t, 512)), sds((8, t, 64), BF16), sds((8, t, 64), BF16),
                 sds((t, 256)), sds((t, 256)), sds((t, 512)), sds((t, 512)),
                 sds((8, t, 64), BF16), sds((t, 128)), sds((t, 128)), sds((2, t, 64), BF16), sds((2, t, 64), BF16),
                 sds((t, 256)), sds((t, 64)), sds((t, 4)), sds((t, 3 * d))]
    names = ["sbq_hm", "sbk", "sbv", "sbk_hm", "sbv_hm", "rq", "rk", "rv", "rg",
             "dq_hm", "dk", "dv", "dk_hm", "dv_hm", "iq", "ik", "iw", "g"]
    outs = pl.pallas_call(
        _inproj_kernel,
        grid=(nblk,),
        in_specs=[tok(d), sh_spec, sc_spec, _full_spec((1, d)), _full_spec(w_packed.shape), _full_spec(bd.shape),
                  _full_spec((1, 512)), _full_spec((1, 512)), _full_spec((1, 512)), _full_spec((1, 128)),
                  pl.BlockSpec((tb, 256), lambda i: (i % rep, 0)), pl.BlockSpec((tb, 256), lambda i: (i % rep, 0))],
        out_specs=out_specs,
        out_shape=out_shape,
        compiler_params=_params(("arbitrary",)),
        name="inproj",
    )(x, sh_arr, sc_arr, norm_g, w_packed, bd, gq, gk, gdq, gdk, cos, sin)
    return dict(zip(names, outs))


def _sbp_kernel(q_ref, k_ref, v_ref, t_ref, o_ref, acc_ref, tot_ref):
    i = pl.program_id(2)
    qb = q_ref.shape[0]
    nd = qb // LANES
    q = q_ref[...]
    tmat = t_ref[...]
    row = lax.broadcasted_iota(jnp.int32, (qb, LANES), 0)
    col = lax.broadcasted_iota(jnp.int32, (qb, LANES), 1)
    acc_ref[...] = jnp.zeros_like(acc_ref)
    tot_ref[...] = jnp.zeros_like(tot_ref)

    def tiles(js, valids):
        offs = [pl.multiple_of(j * LANES, LANES) for j in js]
        zs = [_dot_nt(q, k_ref[pl.ds(off, LANES), :]) * (D_HEAD ** -0.5) for off in offs]
        pre = [_stick_pre(z, valid, tmat) for z, valid in zip(zs, valids)]
        ws, tot = _stick_weights(pre, valids, tot_ref[...])
        tot_ref[...] = tot
        acc = acc_ref[...]
        for a, off in zip(ws, offs):
            acc = acc + _dot(_bf(a), v_ref[pl.ds(off, LANES), :])
        acc_ref[...] = acc

    diag = list(reversed(range(nd)))
    tiles([i * nd + d for d in diag], [col + d * LANES < row for d in diag])
    per = 2 if nd % 2 == 0 else 1

    def body(jj, carry):
        top = i * nd - 1 - jj * per
        tiles([top - p for p in range(per)], [None] * per)
        return carry

    lax.fori_loop(0, i * nd // per, body, 0)
    o_ref[...] = acc_ref[...]


def _sb_prompt(q_hm, k_hm, v_hm, tmat, nb, s, qb):
    h, t, _ = q_hm.shape
    nq = s // qb
    return pl.pallas_call(
        _sbp_kernel,
        grid=(nb, h, nq),
        in_specs=[pl.BlockSpec((None, qb, D_HEAD), lambda b, hh, i: (hh, b * nq + i, 0)),
                  pl.BlockSpec((None, s, D_HEAD), lambda b, hh, i: (hh, b, 0)),
                  pl.BlockSpec((None, s, D_HEAD), lambda b, hh, i: (hh, b, 0)),
                  _full_spec(tmat.shape)],
        out_specs=pl.BlockSpec((None, qb, D_HEAD), lambda b, hh, i: (hh, b * nq + i, 0)),
        out_shape=jax.ShapeDtypeStruct((h, t, D_HEAD), F32),
        scratch_shapes=[pltpu.VMEM((qb, D_HEAD), F32), pltpu.VMEM((qb, LANES), F32)],
        compiler_params=_params(("arbitrary", "arbitrary", "arbitrary")),
        name="sb_prompt",
    )(q_hm, k_hm, v_hm, tmat)


def _ret_log_gamma():
    return np.log1p(-np.exp2(-5.0 - np.arange(RET_HEADS, dtype=np.float32))).astype(np.float32)


def _ret_kernel(q_ref, k_ref, v_ref, g_ref, s0_ref, dec_ref, qd_ref, kd_ref, gn_ref, o_ref, s1_ref, s_scr, *, sdecay):
    c = pl.program_id(1)

    @pl.when(c == 0)
    def _():
        s_scr[...] = s0_ref[...]

    q = q_ref[...]
    k = k_ref[...]
    v = v_ref[...]
    outs = []
    for h in range(RET_HEADS):
        qh = _bf(q[:, h * RET_DK:(h + 1) * RET_DK])
        kh = k[:, h * RET_DK:(h + 1) * RET_DK]
        vh = _bf(v[:, h * RET_DV:(h + 1) * RET_DV])
        s = s_scr[h]
        inner = _dot_nt(qh, _bf(kh)) * dec_ref[h]
        o = _dot(_bf(inner), vh) + _dot(qh, _bf(s)) * qd_ref[h]
        s_scr[h] = sdecay[h] * s + _dot_tn(_bf(kh * kd_ref[h]), vh)
        mu = jnp.mean(o, axis=-1, keepdims=True)
        var = jnp.mean(jnp.square(o - mu), axis=-1, keepdims=True)
        outs.append((o - mu) * lax.rsqrt(var + GN_EPS))
    on = jnp.concatenate(outs, axis=1) * gn_ref[...]
    rg = g_ref[...]
    o_ref[...] = (rg * _sigmoid(rg)) * on

    @pl.when(c == pl.num_programs(1) - 1)
    def _():
        s1_ref[...] = s_scr[...]


def _retention(q, k, v, rg, s0, gn_g, c_eff):
    n, s, _ = q.shape
    cc = RET_CHUNK
    lg = _ret_log_gamma().astype(np.float64)
    idx = np.arange(cc, dtype=np.float64)
    diff = idx[:, None] - idx[None, :]
    dec = np.where(diff[None] >= 0, np.exp(np.maximum(diff, 0.0)[None] * lg[:, None, None]), 0.0)
    qd = np.exp((idx + 1.0)[None, :, None] * lg[:, None, None])
    kd = np.exp((c_eff - 1.0 - idx)[None, :, None] * lg[:, None, None])
    kd = np.where(idx[None, :, None] < c_eff, kd, 0.0)
    sdecay = tuple(float(x) for x in np.exp(c_eff * lg))

    def tok(width):
        return pl.BlockSpec((None, cc, width), lambda i, c: (i, c, 0))

    st = pl.BlockSpec((None, RET_HEADS, RET_DK, RET_DV), lambda i, c: (i, 0, 0, 0))
    return pl.pallas_call(
        functools.partial(_ret_kernel, sdecay=sdecay),
        grid=(n, s // cc),
        in_specs=[tok(256), tok(256), tok(512), tok(512), st,
                  _full_spec((RET_HEADS, cc, cc)), _full_spec((RET_HEADS, cc, 1)), _full_spec((RET_HEADS, cc, 1)),
                  _full_spec((1, RET_V_W))],
        out_specs=[tok(512), st],
        out_shape=[jax.ShapeDtypeStruct((n, s, RET_V_W), F32),
                   jax.ShapeDtypeStruct((n, RET_HEADS, RET_DK, RET_DV), F32)],
        scratch_shapes=[pltpu.VMEM((RET_HEADS, RET_DK, RET_DV), F32)],
        compiler_params=_params(("arbitrary", "arbitrary")),
        name="retention",
    )(q, k, v, rg, s0, jnp.asarray(dec, F32), jnp.asarray(qd, F32), jnp.asarray(kd, F32), gn_g)


def _dsap_kernel(iq_ref, iw_ref, ik_ref, q_ref, k_ref, v_ref, incl_ref, o_ref, *, topk):
    i = pl.program_id(1)
    s = ik_ref.shape[0]
    nq = s // Q_BLOCK

    def attend(wk):
        iq = iq_ref[...]
        iw = iw_ref[...]
        ik = ik_ref[:wk, :]
        score = jnp.zeros((Q_BLOCK, wk), F32)
        for h in range(IDX_HEADS):
            dots = _dot3_nt(iq[:, h * IDX_DIM:(h + 1) * IDX_DIM], ik) * (IDX_DIM ** -0.5)
            score = score + iw[:, h:h + 1] * jnp.maximum(dots, 0.0)
        qpos = i * Q_BLOCK + lax.broadcasted_iota(jnp.int32, (Q_BLOCK, wk), 0)
        kpos = lax.broadcasted_iota(jnp.int32, (Q_BLOCK, wk), 1)
        causal = kpos <= qpos
        score = jnp.where(causal, score + 0.0, NEG_INF)
        sel = (_topk_mask(score, topk, incl_ref[...]) > 0.5) & causal
        for g in range(DSA_KV_HEADS):
            kg = k_ref[g, :wk, :]
            vg = v_ref[g, :wk, :]
            for j in range(DSA_GROUP):
                hh = g * DSA_GROUP + j
                logits = jnp.where(sel, _dot_nt(q_ref[hh], kg) * (D_HEAD ** -0.5), NEG_INF)
                m = jnp.max(logits, axis=-1, keepdims=True)
                p = jnp.exp(logits - m)
                p = p / jnp.sum(p, axis=-1, keepdims=True)
                o_ref[hh] = _dot(_bf(p), vg)

    nbuckets = 4 if (nq % 4 == 0 and s // 4 >= topk) else 1
    per = nq // nbuckets
    for b in range(nbuckets):
        pl.when(i // per == b)(functools.partial(attend, (b + 1) * per * Q_BLOCK))


def _dsa_prompt(iq, iw, ik, q_hm, k_hm, v_hm, incl, nb, s, topk):
    t = iq.shape[0]
    nq = s // Q_BLOCK
    return pl.pallas_call(
        functools.partial(_dsap_kernel, topk=topk),
        grid=(nb, nq),
        in_specs=[pl.BlockSpec((Q_BLOCK, IDX_W), lambda b, i: (b * nq + i, 0)),
                  pl.BlockSpec((Q_BLOCK, IDX_HEADS), lambda b, i: (b * nq + i, 0)),
                  pl.BlockSpec((s, IDX_DIM), lambda b, i: (b, 0)),
                  pl.BlockSpec((DSA_HEADS, Q_BLOCK, D_HEAD), lambda b, i: (0, b * nq + i, 0)),
                  pl.BlockSpec((DSA_KV_HEADS, s, D_HEAD), lambda b, i: (0, b, 0)),
                  pl.BlockSpec((DSA_KV_HEADS, s, D_HEAD), lambda b, i: (0, b, 0)),
                  _full_spec(incl.shape)],
        out_specs=pl.BlockSpec((DSA_HEADS, Q_BLOCK, D_HEAD), lambda b, i: (0, b * nq + i, 0)),
        out_shape=jax.ShapeDtypeStruct((DSA_HEADS, t, D_HEAD), F32),
        compiler_params=_params(("arbitrary", "arbitrary")),
        name="dsa_prompt",
    )(iq, iw, ik, q_hm, k_hm, v_hm, incl)


def _merge_kernel(osb_ref, oret_ref, odsa_ref, g_ref, x_ref, gt_ref, wsb_ref, wret_ref, wdsa_ref, wout_ref, o_ref):
    d = x_ref.shape[1]
    osb = _bf(jnp.concatenate([osb_ref[h] for h in range(SB_HEADS)], axis=1))
    odsa = _bf(jnp.concatenate([odsa_ref[h] for h in range(DSA_HEADS)], axis=1))
    merged = (g_ref[:, :d] * _dot(osb, wsb_ref[...])
              + g_ref[:, d:2 * d] * _dot(_bf(oret_ref[...]), wret_ref[...])
              + g_ref[:, 2 * d:] * _dot(odsa, wdsa_ref[...]))
    o_ref[...] = x_ref[...] + gt_ref[...] * _dot(_bf(merged), wout_ref[...])


def _merge(osb_hm, oret, odsa_hm, g, x, gate, wsb, wret, wdsa, wout, tb):
    t, d = x.shape
    gt_arr, gt_spec = _mod_operand(gate, t, tb)

    def tok(width):
        return pl.BlockSpec((tb, width), lambda i: (i, 0))

    hm = pl.BlockSpec((8, tb, D_HEAD), lambda i: (0, i, 0))
    return pl.pallas_call(
        _merge_kernel,
        grid=(t // tb,),
        in_specs=[hm, tok(512), hm, tok(3 * d), tok(d), gt_spec,
                  _full_spec(wsb.shape), _full_spec(wret.shape), _full_spec(wdsa.shape), _full_spec(wout.shape)],
        out_specs=tok(d),
        out_shape=jax.ShapeDtypeStruct((t, d), F32),
        compiler_params=_params(("arbitrary",)),
        name="merge",
    )(osb_hm, oret, odsa_hm, g, x, gt_arr, wsb, wret, wdsa, wout)


def _top_vals(s, n):
    vals = []
    for _ in range(n):
        m = jnp.max(s, axis=0, keepdims=True)
        vals.append(m)
        s = jnp.where(s == m, NEG_INF, s)
    return vals


def _peer1_kernel(x_ref, sh_ref, sc_ref, ng_ref, wq_ref, keys_ref, h2t_ref, th_ref, e1_ref, s2_ref, e2_ref, q_scr):
    h = pl.program_id(1)

    @pl.when(h == 0)
    def _():
        h2 = _rms_mod(x_ref[...], ng_ref[...], sh_ref[...], sc_ref[...])
        h2t_ref[...] = _bf(h2.T)
        q = _dot(_bf(h2), wq_ref[...])
        for c in range(2 * PEER_HEADS):
            q_scr[c] = q[:, c * PEER_HALF:(c + 1) * PEER_HALF]

    s1 = _dot3_nt(keys_ref[0], q_scr[2 * h])
    s2 = _dot3_nt(keys_ref[1], q_scr[2 * h + 1])
    t1 = _top_vals(s1, PEER_TOPK)
    t2 = jnp.concatenate(_top_vals(s2, PEER_TOPK), axis=0)
    cand = jnp.concatenate([a + t2 for a in t1], axis=0)
    best = _top_vals(cand, PEER_TOPK + 1)
    z = jnp.zeros_like(best[0])
    for b in best[:PEER_TOPK]:
        z = z + jnp.exp(b - best[0])
    thr = 0.5 * (best[PEER_TOPK - 1] + best[PEER_TOPK])
    th_ref[...] = thr - s1
    e1_ref[...] = jnp.exp(s1 - t1[0]) / z
    s2_ref[...] = s2
    e2_ref[...] = jnp.exp(s2 - t2[0:1])


def _peer_scores(x, shift, scale, norm_g, wq, keys, tb):
    t, d = x.shape
    sh_arr, sh_spec = _mod_operand(shift, t, tb)
    sc_arr, sc_spec = _mod_operand(scale, t, tb)
    per_head = pl.BlockSpec((None, PEER_NKEYS, tb), lambda i, h: (h, 0, i))
    sds = jax.ShapeDtypeStruct((PEER_HEADS, PEER_NKEYS, t), F32)
    return pl.pallas_call(
        _peer1_kernel,
        grid=(t // tb, PEER_HEADS),
        in_specs=[pl.BlockSpec((tb, d), lambda i, h: (i, 0)), sh_spec, sc_spec, _full_spec((1, d)),
                  _full_spec(wq.shape),
                  pl.BlockSpec((None, 2, PEER_NKEYS, PEER_HALF), lambda i, h: (h, 0, 0, 0))],
        out_specs=[pl.BlockSpec((d, tb), lambda i, h: (0, i)), per_head, per_head, per_head, per_head],
        out_shape=[jax.ShapeDtypeStruct((d, t), BF16), sds, sds, sds, sds],
        scratch_shapes=[pltpu.VMEM((2 * PEER_HEADS, tb, PEER_HALF), F32)],
        compiler_params=_params(("arbitrary", "arbitrary")),
        name="peer_scores",
    )(x, sh_arr, sc_arr, norm_g, wq, keys)


def _peer2_kernel(h2t_ref, u_ref, vt_ref, th_ref, e1_ref, s2_ref, e2_ref, x_ref, gt_ref, o_ref,
                  acc_ref, thr_ref, e1r_ref, *act_refs):
    j = pl.program_id(1)
    tb = x_ref.shape[0]
    ch = u_ref.shape[0]

    @pl.when(j == 0)
    def _():
        acc_ref[...] = jnp.zeros_like(acc_ref)

    nr = ch // PEER_NKEYS
    wide = min(tb, 2 * LANES)
    sub = 8
    nsub = 2
    for h in range(PEER_HEADS):
        for r in range(nr):
            thr_ref[h, r] = jnp.broadcast_to(th_ref[h, r:r + 1, :], (sub, tb))
            e1r_ref[h, r] = jnp.broadcast_to(e1_ref[h, r:r + 1, :], (sub, tb))
    for cs in range(tb // wide):
        sl = slice(cs * wide, (cs + 1) * wide)
        act_ref = act_refs[cs]
        act_ref[...] = _dot(u_ref[...], h2t_ref[:, sl])
        for k0 in range(0, PEER_NKEYS, sub * nsub):
            w = [[None] * nsub for _ in range(nr)]
            for h in range(PEER_HEADS):
                s2t = [s2_ref[h, k0 + i * sub:k0 + (i + 1) * sub, sl] for i in range(nsub)]
                e2t = [e2_ref[h, k0 + i * sub:k0 + (i + 1) * sub, sl] for i in range(nsub)]
                for r in range(nr):
                    th8 = thr_ref[h, r, :, sl]
                    e18 = e1r_ref[h, r, :, sl]
                    for i in range(nsub):
                        term = jnp.where(s2t[i] >= th8, e2t[i], 0.0) * e18
                        w[r][i] = term if w[r][i] is None else w[r][i] + term
            for r in range(nr):
                for i in range(nsub):
                    rows = slice(r * PEER_NKEYS + k0 + i * sub, r * PEER_NKEYS + k0 + (i + 1) * sub)
                    a = act_ref[rows, :]
                    gelu = 0.5 * a * (1.0 + lax.erf(a * (2.0 ** -0.5)))
                    act_ref[rows, :] = gelu * w[r][i]
        acc_ref[:, sl] += _dot(vt_ref[...], _bf(act_ref[...]))

    @pl.when(j == pl.num_programs(1) - 1)
    def _():
        o_ref[...] = x_ref[...] + gt_ref[...] * acc_ref[...].T


def _peer_experts(h2t, u, vt, th, e1, s2, e2, x, gate, tb, ch):
    t, d = x.shape
    ne = u.shape[0]
    rows = ch // PEER_NKEYS
    wide = min(tb, 2 * LANES)
    gt_arr, gt_spec = _mod_operand(gate, t, tb)
    by_row = pl.BlockSpec((PEER_HEADS, rows, tb), lambda i, j: (0, j, i))
    once = pl.Buffered(1)
    by_tok = pl.BlockSpec((PEER_HEADS, PEER_NKEYS, tb), lambda i, j: (0, 0, i), pipeline_mode=once)
    return pl.pallas_call(
        _peer2_kernel,
        grid=(t // tb, ne // ch),
        in_specs=[pl.BlockSpec((d, tb), lambda i, j: (0, i), pipeline_mode=once),
                  pl.BlockSpec((ch, d), lambda i, j: (j, 0)),
                  pl.BlockSpec((d, ch), lambda i, j: (0, j)),
                  by_row, by_row, by_tok, by_tok,
                  pl.BlockSpec((tb, d), lambda i, j: (i, 0), pipeline_mode=once), gt_spec],
        out_specs=pl.BlockSpec((tb, d), lambda i, j: (i, 0)),
        out_shape=jax.ShapeDtypeStruct((t, d), F32),
        scratch_shapes=[pltpu.VMEM((d, tb), F32),
                        pltpu.VMEM((PEER_HEADS, rows, 8, tb), F32), pltpu.VMEM((PEER_HEADS, rows, 8, tb), F32)]
                       + [pltpu.VMEM((ch, wide), F32)] * (tb // wide),
        compiler_params=_params(("arbitrary", "arbitrary")),
        name="peer_experts",
    )(h2t, u, vt, th, e1, s2, e2, x, gt_arr)


def _pages_per_step(npages, want):
    g = want
    while npages % g:
        g //= 2
    return g


def _dot3(a, b):
    ah, al = _split(a)
    bh, bl = _split(b)
    return _dot(ah, bh) + _dot(ah, bl) + _dot(al, bh)


def _idx_scores(iq, iw, ikt, nt):
    sc = jnp.maximum(_dot3(iq, ikt) * (IDX_DIM ** -0.5), 0.0) * iw
    score = sc[0:nt]
    for h in range(1, IDX_HEADS):
        score = score + sc[h * nt:(h + 1) * nt]
    return score


def _sscore_kernel(pt_ref, iq_ref, iw_ref, in_ref, *refs, g):
    page_refs, (past_ref, new_ref) = refs[:g], refs[g:]
    s = pl.program_id(1)
    nt = new_ref.shape[0]
    iq = iq_ref[...]
    iw = iw_ref[...]

    @pl.when(s == 0)
    def _():
        new_ref[...] = _idx_scores(iq, iw, in_ref[...], nt)

    @pl.when(s > 0)
    def _():
        for p in range(g):
            past_ref[:, p * LANES:(p + 1) * LANES] = _idx_scores(iq, iw, page_refs[p][...], nt)


def _sample_scores(page_table, iq16, iw16, ikt_cache, ikt_new, layer, g):
    n, npages = page_table.shape
    nt = iq16.shape[1] // IDX_HEADS

    def page_spec(p):
        return pl.BlockSpec((None, None, IDX_DIM, LANES),
                            lambda i, s, pt: (layer, pt[i, (jnp.maximum(s, 1) - 1) * g + p], 0, 0))

    def per_n(rows, width):
        return pl.BlockSpec((None, rows, width), lambda i, s, pt: (i, 0, 0))

    grid_spec = pltpu.PrefetchScalarGridSpec(
        num_scalar_prefetch=1,
        grid=(n, npages // g + 1),
        in_specs=[per_n(IDX_HEADS * nt, IDX_DIM), per_n(IDX_HEADS * nt, 1), per_n(IDX_DIM, LANES)]
                 + [page_spec(p) for p in range(g)],
        out_specs=[pl.BlockSpec((None, nt, g * LANES), lambda i, s, pt: (i, 0, jnp.maximum(s, 1) - 1)),
                   per_n(nt, LANES)],
    )
    return pl.pallas_call(
        functools.partial(_sscore_kernel, g=g),
        grid_spec=grid_spec,
        out_shape=[jax.ShapeDtypeStruct((n, nt, npages * LANES), F32), jax.ShapeDtypeStruct((n, nt, LANES), F32)],
        compiler_params=_params(("arbitrary", "arbitrary")),
        name="sample_scores",
    )(page_table, iq16, iw16, ikt_new, *([ikt_cache] * g))


def _sattn_kernel(pt_ref, qsb_ref, qd_ref, t_ref, kn_ref, vn_ref, dkn_ref, dvn_ref, mn_ref, mp_ref, *refs, g):
    k_refs, v_refs, dk_refs, dv_refs = refs[:g], refs[g:2 * g], refs[2 * g:3 * g], refs[3 * g:4 * g]
    osb_ref, od_ref, acc_ref, tot_ref, mx_ref, l_ref, dacc_ref = refs[4 * g:]
    s = pl.program_id(1)
    nrow = qsb_ref.shape[0]
    drow = qd_ref.shape[0]
    qsb = qsb_ref[...]
    qd = qd_ref[...]
    tmat = t_ref[...]

    @pl.when(s == 0)
    def _():
        acc_ref[...] = jnp.zeros_like(acc_ref)
        tot_ref[...] = jnp.zeros_like(tot_ref)
        mx_ref[...] = jnp.full_like(mx_ref, -1e30)
        l_ref[...] = jnp.zeros_like(l_ref)
        dacc_ref[...] = jnp.zeros_like(dacc_ref)

    def attend(pages, sel, valid):
        zs = [_dot(qsb, _bf(kt)) * (D_HEAD ** -0.5) for kt, _, _, _ in pages]
        logits = [_dot(qd, _bf(dkt)) * (D_HEAD ** -0.5) for _, _, dkt, _ in pages]
        valids = [valid] * len(pages)
        pre = [_stick_pre(z, valid, tmat) for z in zs]
        ws, tot = _stick_weights(pre, valids, tot_ref[...])
        tot_ref[...] = tot
        acc = acc_ref[...]
        for a, (_, vt, _, _) in zip(ws, pages):
            acc = acc + _dot_nt(_bf(a), _bf(vt))
        acc_ref[...] = acc
        logits = jnp.concatenate(logits, axis=1)
        m_old = mx_ref[...]
        m_new = jnp.maximum(m_old, jnp.max(jnp.where(sel, logits, -1e30), axis=-1, keepdims=True))
        p = jnp.where(sel, jnp.exp(logits - m_new), 0.0)
        alpha = jnp.exp(m_old - m_new)
        mx_ref[...] = m_new
        l_ref[...] = alpha * l_ref[...] + jnp.sum(p, axis=-1, keepdims=True)
        dacc = alpha * dacc_ref[...]
        for c, (_, _, _, dvt) in enumerate(pages):
            dacc = dacc + _dot_nt(_bf(p[:, c * LANES:(c + 1) * LANES]), _bf(dvt))
        dacc_ref[...] = dacc

    def tile_rows(m8):
        return jnp.concatenate([m8] * (drow // m8.shape[0]), axis=0) > 0.5

    @pl.when(s == 0)
    def _():
        trow = lax.broadcasted_iota(jnp.int32, (nrow, LANES), 0) // SB_HEADS
        col = lax.broadcasted_iota(jnp.int32, (nrow, LANES), 1)
        attend([(kn_ref[...], vn_ref[...], dkn_ref[...], dvn_ref[...])], tile_rows(mn_ref[...]), col < trow)

    @pl.when(s > 0)
    def _():
        mp = mp_ref[...]
        sel = tile_rows(jnp.concatenate([mp[:, (g - 1 - p) * LANES:(g - p) * LANES] for p in range(g)], axis=1))
        attend([(k_refs[p][...], v_refs[p][...], dk_refs[p][...], dv_refs[p][...]) for p in range(g)], sel, None)

    @pl.when(s == pl.num_programs(1) - 1)
    def _():
        acc = acc_ref[...]
        hrow = lax.broadcasted_iota(jnp.int32, acc.shape, 0) % SB_HEADS
        hcol = lax.broadcasted_iota(jnp.int32, acc.shape, 1) // D_HEAD
        own = jnp.where(hrow == hcol, acc, 0.0)
        osb_ref[...] = jnp.sum(own.reshape(nrow // SB_HEADS, SB_HEADS, acc.shape[1]), axis=1)
        dacc = dacc_ref[...] / l_ref[...]
        half = drow // DSA_KV_HEADS
        grow = lax.broadcasted_iota(jnp.int32, dacc.shape, 0) // half
        gcol = lax.broadcasted_iota(jnp.int32, dacc.shape, 1) // D_HEAD
        down = jnp.where(grow == gcol, dacc, 0.0)
        od_ref[...] = down[:half] + down[half:]


def _sample_attention(page_table, qsb, qd, tmat, kt_new, vt_new, dkt_new, dvt_new, mask_new, mask_past,
                      kt_cache, vt_cache, dkt_cache, dvt_cache, layer, g):
    n, npages = page_table.shape
    nrow = qsb.shape[1]
    drow = qd.shape[1]
    nsteps = npages // g

    def page_spec(width, p):
        return pl.BlockSpec((None, None, width, LANES),
                            lambda i, s, pt: (layer, pt[i, npages - 1 - (jnp.maximum(s, 1) - 1) * g - p], 0, 0))

    def per_n(rows, width):
        return pl.BlockSpec((None, rows, width), lambda i, s, pt: (i, 0, 0))

    in_specs = ([per_n(nrow, SB_W), per_n(drow, DSA_KV_W), pl.BlockSpec(tmat.shape, lambda i, s, pt: (0, 0)),
                 per_n(SB_W, LANES), per_n(SB_W, LANES), per_n(DSA_KV_W, LANES), per_n(DSA_KV_W, LANES),
                 per_n(8, LANES),
                 pl.BlockSpec((None, 8, g * LANES), lambda i, s, pt: (i, 0, nsteps - jnp.maximum(s, 1)))]
                + [page_spec(SB_W, p) for p in range(g)] + [page_spec(SB_W, p) for p in range(g)]
                + [page_spec(DSA_KV_W, p) for p in range(g)] + [page_spec(DSA_KV_W, p) for p in range(g)])
    grid_spec = pltpu.PrefetchScalarGridSpec(
        num_scalar_prefetch=1,
        grid=(n, nsteps + 1),
        in_specs=in_specs,
        out_specs=[per_n(nrow // SB_HEADS, SB_W), per_n(drow // DSA_KV_HEADS, DSA_KV_W)],
        scratch_shapes=[pltpu.VMEM((nrow, SB_W), F32), pltpu.VMEM((nrow, LANES), F32),
                        pltpu.VMEM((drow, 1), F32), pltpu.VMEM((drow, 1), F32), pltpu.VMEM((drow, DSA_KV_W), F32)],
    )
    return pl.pallas_call(
        functools.partial(_sattn_kernel, g=g),
        grid_spec=grid_spec,
        out_shape=[jax.ShapeDtypeStruct((n, nrow // SB_HEADS, SB_W), F32),
                   jax.ShapeDtypeStruct((n, drow // DSA_KV_HEADS, DSA_KV_W), F32)],
        compiler_params=_params(("arbitrary", "arbitrary")),
        name="sample_attention",
    )(page_table, qsb, qd, tmat, kt_new, vt_new, dkt_new, dvt_new, mask_new, mask_past,
      *([kt_cache] * g), *([vt_cache] * g), *([dkt_cache] * g), *([dvt_cache] * g))


def _sthr_kernel(sc_ref, incl_ref, m_ref, *, topk, past, nt):
    rows, width = sc_ref.shape
    t = (pl.program_id(0) * rows + lax.broadcasted_iota(jnp.int32, (rows, width), 0)) % nt
    pos = lax.broadcasted_iota(jnp.int32, (rows, width), 1)
    causal = pos <= past + t
    score = jnp.where(causal, sc_ref[...] + 0.0, NEG_INF)
    m_ref[...] = jnp.where(causal, _topk_mask(score, topk, incl_ref[...]), 0.0)


def _sample_select(scores, incl, topk, past, nt):
    r, width = scores.shape
    rb = 8
    return pl.pallas_call(
        functools.partial(_sthr_kernel, topk=topk, past=past, nt=nt),
        grid=(r // rb,),
        in_specs=[pl.BlockSpec((rb, width), lambda i: (i, 0)), _full_spec(incl.shape)],
        out_specs=pl.BlockSpec((rb, width), lambda i: (i, 0)),
        out_shape=jax.ShapeDtypeStruct((r, width), F32),
        compiler_params=_params(("arbitrary",)),
        name="sample_select",
    )(scores, incl)


def _rope_tables(pos):
    half = D_HEAD // 2
    freqs = ROPE_BASE ** (-jnp.arange(half, dtype=F32) / half)
    ang = pos.astype(F32)[:, None] * freqs[None, :]
    cos = jnp.cos(ang)
    sin = jnp.sin(ang)
    return (jnp.tile(jnp.concatenate([cos, cos], axis=1), (1, RET_HEADS)),
            jnp.tile(jnp.concatenate([-sin, sin], axis=1), (1, RET_HEADS)))


def _tri_constants():
    j = np.arange(LANES)
    suffix = (j[:, None] > j[None, :]).astype(np.float32)
    tmat = np.concatenate([suffix, np.ones((LANES, LANES), np.float32)], axis=1)
    incl = (j[:, None] <= j[None, :]).astype(np.float32)
    blk = np.arange(SB_W) // D_HEAD
    bd = (blk[:, None] == blk[None, :]).astype(np.float32)
    return jnp.asarray(tmat, BF16), jnp.asarray(incl, BF16), jnp.asarray(bd, BF16)


def _pad_rows(a, rows):
    return jnp.pad(a, ((0, 0), (0, rows - a.shape[1]), (0, 0)))


def kernel(x_prompt, x_sample, cache_sb_k, cache_sb_v, cache_dsa_k, cache_dsa_v, cache_idx_k, state_ret, page_table, c_prompt, c_sample, norm1_g, ada1_w, ada1_b, w_in, sb_qn_g, sb_kn_g, dsa_qn_g, dsa_kn_g, ret_gn_g, w_br_sb, w_br_ret, w_br_dsa, w_out, norm2_g, ada2_w, ada2_b, peer_wq, peer_subkeys, peer_u, peer_v):
    nb, s_len, d = x_prompt.shape
    ns, t_len, _ = x_sample.shape
    depth = w_in.shape[0]
    npool, page = cache_sb_k.shape[1], cache_sb_k.shape[2]
    npages = page_table.shape[1]
    past = npages * page
    tp = nb * s_len
    ts = ns * t_len
    topk_p = min(DSA_TOPK_MAX, s_len // 4)
    topk_s = min(DSA_TOPK_MAX, (past + t_len) // 4)
    tb_p = 256
    tb_peer = next(c for c in (1024, 512, 256) if tp % c == 0)
    peer_ch = 1024
    sb_qb = 512 if s_len % 512 == 0 else Q_BLOCK

    tmat, incl, bd = _tri_constants()
    cos_p, sin_p = _rope_tables(jnp.arange(s_len))
    cos_s, sin_s = _rope_tables(jnp.tile(past + jnp.arange(t_len), ns))

    c_all = jnp.concatenate([c_prompt, c_sample], axis=0)
    ada1 = _ada(c_all, ada1_w, ada1_b)
    ada2 = _ada(c_all, ada2_w, ada2_b)

    assert page == LANES and 8 % t_len == 0
    kt_sb = cache_sb_k.transpose(0, 1, 3, 4, 2).reshape(depth, npool, SB_W, page)
    vt_sb = cache_sb_v.transpose(0, 1, 3, 4, 2).reshape(depth, npool, SB_W, page)
    kt_dsa = cache_dsa_k.transpose(0, 1, 3, 4, 2).reshape(depth, npool, DSA_KV_W, page)
    vt_dsa = cache_dsa_v.transpose(0, 1, 3, 4, 2).reshape(depth, npool, DSA_KV_W, page)
    ikt_cache = cache_idx_k.transpose(0, 1, 3, 2)
    eye_h = jnp.eye(SB_HEADS, dtype=F32)
    eye_g = jnp.eye(DSA_KV_HEADS, dtype=F32)
    g_attn = _pages_per_step(npages, 8)
    g_score = _pages_per_step(npages, 16)

    def new_keys_t(a, width):
        a = a.reshape(ns, t_len, width).transpose(0, 2, 1)
        return jnp.pad(a, ((0, 0), (0, 0), (0, page - t_len)))

    xp = x_prompt.reshape(tp, d)
    xs = x_sample.reshape(ts, d)
    outs = {k: [] for k in ("p_sb_k", "p_sb_v", "p_dsa_k", "p_dsa_v", "p_idx_k", "p_ret",
                            "s_sb_k", "s_sb_v", "s_dsa_k", "s_dsa_v", "s_idx_k", "s_ret")}
    for l in range(depth):
        w_packed = _pack_w_in(w_in[l], d)
        tile8 = lambda g: jnp.tile(g, 8).reshape(1, 512)
        gains = (tile8(sb_qn_g[l]), tile8(sb_kn_g[l]), tile8(dsa_qn_g[l]), jnp.tile(dsa_kn_g[l], 2).reshape(1, 128))
        wsb, wret, wdsa, wout = _bf(w_br_sb[l]), _bf(w_br_ret[l]), _bf(w_br_dsa[l]), _bf(w_out[l])
        wq = _bf(peer_wq[l])
        u_bf = _bf(peer_u[l])
        vt_bf = _bf(peer_v[l]).T
        n1g = norm1_g[l].reshape(1, d)
        n2g = norm2_g[l].reshape(1, d)
        gn_g = ret_gn_g[l].reshape(1, RET_V_W)

        def split3(a):
            return a[:, :d], a[:, d:2 * d], a[:, 2 * d:]

        sh1p, sc1p, gt1p = split3(ada1[l, :nb])
        sh1s, sc1s, gt1s = split3(ada1[l, nb:])
        sh2p, sc2p, gt2p = split3(ada2[l, :nb])
        sh2s, sc2s, gt2s = split3(ada2[l, nb:])

        m = _inproj(xp, sh1p, sc1p, n1g, w_packed, bd, *gains, cos_p, sin_p, tb_p)
        o_sb = _sb_prompt(m["sbq_hm"], m["sbk_hm"], m["sbv_hm"], tmat, nb, s_len, sb_qb)
        o_ret, st_p = _retention(m["rq"].reshape(nb, s_len, -1), m["rk"].reshape(nb, s_len, -1),
                                 m["rv"].reshape(nb, s_len, -1), m["rg"].reshape(nb, s_len, -1),
                                 jnp.zeros((nb, RET_HEADS, RET_DK, RET_DV), F32), gn_g, RET_CHUNK)
        o_dsa = _dsa_prompt(m["iq"], m["iw"], m["ik"], m["dq_hm"], m["dk_hm"], m["dv_hm"], incl, nb, s_len, topk_p)
        xp = _merge(o_sb, o_ret.reshape(tp, -1), o_dsa, m["g"], xp, gt1p, wsb, wret, wdsa, wout, tb_p)
        h2t, th, e1, s2, e2 = _peer_scores(xp, sh2p, sc2p, n2g, wq, peer_subkeys[l], tb_p)
        xp = _peer_experts(h2t, u_bf, vt_bf, th, e1, s2, e2, xp, gt2p, tb_peer, peer_ch)
        outs["p_sb_k"].append(m["sbk"].reshape(nb, s_len, SB_HEADS, D_HEAD))
        outs["p_sb_v"].append(m["sbv"].reshape(nb, s_len, SB_HEADS, D_HEAD))
        outs["p_dsa_k"].append(m["dk"].reshape(nb, s_len, DSA_KV_HEADS, D_HEAD))
        outs["p_dsa_v"].append(m["dv"].reshape(nb, s_len, DSA_KV_HEADS, D_HEAD))
        outs["p_idx_k"].append(m["ik"].reshape(nb, s_len, IDX_DIM))
        outs["p_ret"].append(st_p)

        m = _inproj(xs, sh1s, sc1s, n1g, w_packed, bd, *gains, cos_s, sin_s, ts)
        q_sb = m["sbq_hm"].astype(F32).reshape(SB_HEADS, ns, t_len, D_HEAD)
        qbd = jnp.einsum("hntd,hg->ntghd", q_sb, eye_h).reshape(ns, t_len * SB_HEADS, SB_W)
        iq16 = m["iq"].reshape(ns, t_len, IDX_HEADS, IDX_DIM).transpose(0, 2, 1, 3).reshape(ns, IDX_HEADS * t_len, IDX_DIM)
        iw16 = m["iw"].reshape(ns, t_len, IDX_HEADS).transpose(0, 2, 1).reshape(ns, IDX_HEADS * t_len, 1)
        sc_past, sc_new = _sample_scores(page_table, iq16, iw16, ikt_cache, new_keys_t(m["ik"], IDX_DIM), l, g_score)
        scores = jnp.concatenate([sc_past, sc_new], axis=-1)
        width = scores.shape[-1]
        mask = _sample_select(scores.reshape(ts, width), incl, topk_s, past, t_len).reshape(ns, t_len, width)
        mask8 = jnp.tile(mask, (1, 8 // t_len, 1))
        dq = m["dq_hm"].astype(F32).reshape(DSA_KV_HEADS, DSA_GROUP, ns, t_len, D_HEAD)
        qd = jnp.einsum("gjntd,gk->ngjtkd", dq, eye_g).reshape(ns, DSA_HEADS * t_len, DSA_KV_W)
        o_sb_s, o_dsa_s = _sample_attention(
            page_table, _bf(qbd), _bf(qd), tmat,
            new_keys_t(m["sbk"], SB_W), new_keys_t(m["sbv"], SB_W),
            new_keys_t(m["dk"], DSA_KV_W), new_keys_t(m["dv"], DSA_KV_W),
            mask8[:, :, past:], mask8[:, :, :past], kt_sb, vt_sb, kt_dsa, vt_dsa, l, g_attn)
        pad = lambda a: _pad_rows(a.reshape(ns, t_len, -1), RET_CHUNK)
        o_ret_s, st_s = _retention(pad(m["rq"]), pad(m["rk"]), pad(m["rv"]), pad(m["rg"]), state_ret[l], gn_g, t_len)
        o_sb_hm = o_sb_s.reshape(ts, SB_HEADS, D_HEAD).transpose(1, 0, 2)
        o_dsa_hm = o_dsa_s.reshape(ns, DSA_GROUP, t_len, DSA_KV_HEADS, D_HEAD).transpose(3, 1, 0, 2, 4)
        o_dsa_hm = o_dsa_hm.reshape(DSA_HEADS, ts, D_HEAD)
        xs = _merge(o_sb_hm, o_ret_s[:, :t_len].reshape(ts, -1), o_dsa_hm, m["g"], xs, gt1s, wsb, wret, wdsa, wout, ts)
        h2t, th, e1, s2, e2 = _peer_scores(xs, sh2s, sc2s, n2g, wq, peer_subkeys[l], ts)
        xs = _peer_experts(h2t, u_bf, vt_bf, th, e1, s2, e2, xs, gt2s, ts, peer_ch)
        outs["s_sb_k"].append(m["sbk"].reshape(ns, t_len, SB_HEADS, D_HEAD))
        outs["s_sb_v"].append(m["sbv"].reshape(ns, t_len, SB_HEADS, D_HEAD))
        outs["s_dsa_k"].append(m["dk"].reshape(ns, t_len, DSA_KV_HEADS, D_HEAD))
        outs["s_dsa_v"].append(m["dv"].reshape(ns, t_len, DSA_KV_HEADS, D_HEAD))
        outs["s_idx_k"].append(m["ik"].reshape(ns, t_len, IDX_DIM))
        outs["s_ret"].append(st_s)

    st = {k: jnp.stack(v) for k, v in outs.items()}
    return (xp.reshape(nb, s_len, d), xs.reshape(ns, t_len, d),
            st["p_sb_k"], st["p_sb_v"], st["p_dsa_k"], st["p_dsa_v"], st["p_idx_k"], st["p_ret"],
            st["s_sb_k"], st["s_sb_v"], st["s_dsa_k"], st["s_dsa_v"], st["s_idx_k"], st["s_ret"])
```

```python
import functools

import numpy as np
import jax
import jax.numpy as jnp
from jax import lax
from jax.experimental import pallas as pl
from jax.experimental.pallas import tpu as pltpu

F32 = jnp.float32
BF16 = jnp.bfloat16

D_HEAD = 64
SB_HEADS = 8
SB_W = SB_HEADS * D_HEAD
RET_HEADS = 4
RET_DK = 64
RET_DV = 128
RET_QK_W = RET_HEADS * RET_DK
RET_V_W = RET_HEADS * RET_DV
RET_CHUNK = 128
DSA_HEADS = 8
DSA_KV_HEADS = 2
DSA_GROUP = DSA_HEADS // DSA_KV_HEADS
DSA_W = DSA_HEADS * D_HEAD
DSA_KV_W = DSA_KV_HEADS * D_HEAD
IDX_HEADS = 4
IDX_DIM = 64
IDX_W = IDX_HEADS * IDX_DIM
DSA_TOPK_MAX = 256
Q_BLOCK = 128
N_BRANCH = 3
PEER_HEADS = 8
PEER_NKEYS = 128
PEER_EXPERTS = PEER_NKEYS * PEER_NKEYS
PEER_DQ = 256
PEER_HALF = PEER_DQ // 2
PEER_TOPK = 16
RMS_EPS = 1e-6
GN_EPS = 1e-5
ROPE_BASE = 10000.0

LANES = 128
VMEM_LIMIT = 56 * 1024 * 1024
INT_MIN = np.int32(-2 ** 31)
NEG_INF = float("-inf")


def _bf(x):
    return x.astype(BF16)


def _split(x):
    hi = x.astype(BF16)
    lo = (x - hi.astype(F32)).astype(BF16)
    return hi, lo


def _dot(a, b):
    return jnp.dot(a, b, preferred_element_type=F32)


def _dot_nt(a, b):
    return lax.dot_general(a, b, (((1,), (1,)), ((), ())), preferred_element_type=F32)


def _dot_tn(a, b):
    return lax.dot_general(a, b, (((0,), (0,)), ((), ())), preferred_element_type=F32)


def _dot3_nt(a, b):
    ah, al = _split(a)
    bh, bl = _split(b)
    return _dot_nt(ah, bh) + _dot_nt(ah, bl) + _dot_nt(al, bh)


def _sigmoid(x):
    return 1.0 / (1.0 + jnp.exp(-x))


def _params(sem):
    return pltpu.CompilerParams(dimension_semantics=sem, vmem_limit_bytes=VMEM_LIMIT)


def _full_spec(shape):
    nd = len(shape)
    return pl.BlockSpec(shape, lambda *_: (0,) * nd, pipeline_mode=pl.Buffered(1))


def _mod_operand(mod, n_tok, tb):
    g, d = mod.shape
    tg = n_tok // g
    if tg % tb == 0:
        per = tg // tb
        return mod.reshape(g, 1, d), pl.BlockSpec((None, 1, d), lambda i, *_: (i // per, 0, 0))
    return jnp.repeat(mod, tg, axis=0), pl.BlockSpec((tb, d), lambda i, *_: (i, 0))


def _rms_mod(x, g, shift, scale):
    ms = jnp.mean(x * x, axis=-1, keepdims=True)
    return (x * lax.rsqrt(ms + RMS_EPS) * g) * (1.0 + scale) + shift


def _sortable_key(score):
    bits = lax.bitcast_convert_type(score, jnp.int32)
    return jnp.where(bits < 0, bits ^ np.int32(0x7FFFFFFF), bits)


def _kth_largest_key(key, k):
    rows = key.shape[0]

    def body(b, ans):
        cand = ans | jnp.left_shift(jnp.int32(1), 31 - b)
        cnt = jnp.sum(jnp.where(key >= (cand ^ INT_MIN), 1.0, 0.0), axis=1, keepdims=True)
        return jnp.where(cnt >= k, cand, ans)

    ans = lax.fori_loop(0, 32, body, jnp.zeros((rows, 1), jnp.int32))
    return ans ^ INT_MIN


def _topk_mask(score, k, incl):
    key = _sortable_key(score)
    thr = _kth_largest_key(key, k)
    gt = key > thr
    tie = key == thr
    need = k - jnp.sum(jnp.where(gt, 1.0, 0.0), axis=1, keepdims=True)
    out = []
    off = jnp.zeros_like(need)
    for c in range(score.shape[1] // LANES):
        sl = slice(c * LANES, (c + 1) * LANES)
        tie_c = tie[:, sl]
        rank = _dot(jnp.where(tie_c, 1.0, 0.0).astype(BF16), incl) + off
        out.append(jnp.where(gt[:, sl] | (tie_c & (rank <= need)), 1.0, 0.0))
        off = rank[:, LANES - 1:LANES]
    return jnp.concatenate(out, axis=1)


def _stick_pre(z, valid, tmat):
    l1p = jnp.log1p(jnp.exp(-jnp.abs(z)))
    ls = jnp.minimum(z, 0.0) - l1p
    lk = -jnp.maximum(z, 0.0) - l1p
    if valid is not None:
        lk = jnp.where(valid, lk, 0.0)
    hi, lo = _split(lk)
    return ls, _dot(hi, tmat) + _dot(lo, tmat)


def _stick_weights(pre, valids, tot):
    out = []
    for (ls, mm), valid in zip(pre, valids):
        a = jnp.exp(ls + mm[:, :LANES] + tot)
        if valid is not None:
            a = jnp.where(valid, a, 0.0)
        out.append(a)
        tot = tot + mm[:, LANES:]
    return out, tot


def _ada_kernel(c_ref, w_ref, b_ref, o_ref):
    c = c_ref[...]
    o_ref[...] = _dot(_bf(c * _sigmoid(c)), _bf(w_ref[...])) + b_ref[...]


def _ada(c, w, b):
    n, d = c.shape
    nl, _, w3 = w.shape
    tn = 1024
    return pl.pallas_call(
        _ada_kernel,
        grid=(nl, w3 // tn),
        in_specs=[pl.BlockSpec((n, d), lambda l, j: (0, 0)),
                  pl.BlockSpec((None, d, tn), lambda l, j: (l, 0, j)),
                  pl.BlockSpec((None, 1, tn), lambda l, j: (l, 0, j))],
        out_specs=pl.BlockSpec((None, n, tn), lambda l, j: (l, 0, j)),
        out_shape=jax.ShapeDtypeStruct((nl, n, w3), F32),
        compiler_params=_params(("arbitrary", "arbitrary")),
        name="ada",
    )(c, w, b.reshape(nl, 1, w3))


_C_SBQ, _C_SBK, _C_SBV = 0, 512, 1024
_C_RQ, _C_RK, _C_RV, _C_RG = 1536, 1792, 2048, 2560
_C_DQ, _C_DK, _C_DV, _C_IQ = 3072, 3584, 3712, 3840
_C_ALIGNED = 4096


def _pack_w_in(w):
    a = w[:, :_C_ALIGNED]
    iw = w[:, _C_ALIGNED:_C_ALIGNED + IDX_HEADS]
    ik = w[:, _C_ALIGNED + IDX_HEADS:_C_ALIGNED + IDX_HEADS + IDX_DIM]
    g = w[:, _C_ALIGNED + IDX_HEADS + IDX_DIM:]
    pad = jnp.zeros((w.shape[0], LANES - IDX_DIM - IDX_HEADS), w.dtype)
    return a.astype(BF16), jnp.concatenate([ik, iw, pad], axis=1).astype(BF16), g.astype(BF16)


def _head_rms(y, bd, g):
    hi, lo = _split(y * y)
    ms = (_dot(hi, bd) + _dot(lo, bd)) * (1.0 / D_HEAD)
    return y * lax.rsqrt(ms + RMS_EPS) * g


def _rotary(y, cos, sin, first_half):
    w = y.shape[1]
    partner = jnp.where(first_half, pltpu.roll(y, w - D_HEAD // 2, 1), pltpu.roll(y, D_HEAD // 2, 1))
    return y * cos + partner * sin


def _inproj_kernel(x_ref, sh_ref, sc_ref, ng_ref, w_ref, wi_ref, wg_ref, bd_ref, gq_ref, gk_ref, gdq_ref, gdk_ref,
                   cos_ref, sin_ref,
                   sbq_hm, sbk_o, sbv_o, sbk_hm, sbv_hm, rq_o, rk_o, rv_o, rg_o,
                   dq_hm, dk_o, dv_o, dk_hm, dv_hm, iq_o, ik_o, iw_o, g_o):
    h = _bf(_rms_mod(x_ref[...], ng_ref[...], sh_ref[...], sc_ref[...]))

    def proj(c0, width):
        return _dot(h, w_ref[:, c0:c0 + width])

    def to_heads(y, ref, nh):
        for i in range(nh):
            ref[i] = y[:, i * D_HEAD:(i + 1) * D_HEAD].astype(ref.dtype)

    bd = bd_ref[...]
    to_heads(_head_rms(proj(_C_SBQ, SB_W), bd, gq_ref[...]), sbq_hm, SB_HEADS)
    sbk = _head_rms(proj(_C_SBK, SB_W), bd, gk_ref[...])
    sbk_o[...] = sbk
    to_heads(sbk, sbk_hm, SB_HEADS)
    sbv = proj(_C_SBV, SB_W)
    sbv_o[...] = sbv
    to_heads(sbv, sbv_hm, SB_HEADS)

    cos = cos_ref[...]
    sin = sin_ref[...]
    lane = lax.broadcasted_iota(jnp.int32, cos.shape, 1)
    first_half = (lane % D_HEAD) < (D_HEAD // 2)
    rq_o[...] = _rotary(proj(_C_RQ, RET_QK_W), cos, sin, first_half)
    rk_o[...] = _rotary(proj(_C_RK, RET_QK_W), cos, sin, first_half) * (RET_DK ** -0.5)
    rv_o[...] = proj(_C_RV, RET_V_W)
    rg_o[...] = proj(_C_RG, RET_V_W)

    to_heads(_head_rms(proj(_C_DQ, DSA_W), bd, gdq_ref[...]), dq_hm, DSA_HEADS)
    dk = _head_rms(proj(_C_DK, DSA_KV_W), bd_ref[:DSA_KV_W, :DSA_KV_W], gdk_ref[...])
    dk_o[...] = dk
    to_heads(dk, dk_hm, DSA_KV_HEADS)
    dv = proj(_C_DV, DSA_KV_W)
    dv_o[...] = dv
    to_heads(dv, dv_hm, DSA_KV_HEADS)

    iq_o[...] = proj(_C_IQ, IDX_W)
    ikw = _dot(h, wi_ref[...])
    ik_o[...] = ikw[:, :IDX_DIM]
    iw_o[...] = ikw[:, IDX_DIM:IDX_DIM + IDX_HEADS] * (IDX_HEADS ** -0.5)
    g_o[...] = _sigmoid(_dot(h, wg_ref[...]))


def _inproj(x, shift, scale, norm_g, w_packed, bd, gq, gk, gdq, gdk, cos, sin, tb):
    t, d = x.shape
    nblk = t // tb
    rep = cos.shape[0] // tb
    sh_arr, sh_spec = _mod_operand(shift, t, tb)
    sc_arr, sc_spec = _mod_operand(scale, t, tb)

    def tok(width):
        return pl.BlockSpec((tb, width), lambda i: (i, 0))

    def hm(nh):
        return pl.BlockSpec((nh, tb, D_HEAD), lambda i: (0, i, 0))

    def sds(shape, dt=F32):
        return jax.ShapeDtypeStruct(shape, dt)

    out_specs = [hm(8), tok(512), tok(512), hm(8), hm(8), tok(256), tok(256), tok(512), tok(512),
                 hm(8), tok(128), tok(128), hm(2), hm(2), tok(256), tok(64), tok(4), tok(3 * d)]
    out_shape = [sds((8, t, 64), BF16), sds((t, 512)), sds((t, 512)), sds((8, t, 64), BF16), sds((8, t, 64), BF16),
                 sds((t, 256)), sds((t, 256)), sds((t, 512)), sds((t, 512)),
                 sds((8, t, 64), BF16), sds((t, 128)), sds((t, 128)), sds((2, t, 64), BF16), sds((2, t, 64), BF16),
                 sds((t, 256)), sds((t, 64)), sds((t, 4)), sds((t, 3 * d))]
    names = ["sbq_hm", "sbk", "sbv", "sbk_hm", "sbv_hm", "rq", "rk", "rv", "rg",
             "dq_hm", "dk", "dv", "dk_hm", "dv_hm", "iq", "ik", "iw", "g"]
    outs = pl.pallas_call(
        _inproj_kernel,
        grid=(nblk,),
        in_specs=[tok(d), sh_spec, sc_spec, _full_spec((1, d)), *[_full_spec(w.shape) for w in w_packed], _full_spec(bd.shape),
                  _full_spec((1, 512)), _full_spec((1, 512)), _full_spec((1, 512)), _full_spec((1, 128)),
                  pl.BlockSpec((tb, 256), lambda i: (i % rep, 0)), pl.BlockSpec((tb, 256), lambda i: (i % rep, 0))],
        out_specs=out_specs,
        out_shape=out_shape,
        compiler_params=_params(("arbitrary",)),
        name="inproj",
    )(x, sh_arr, sc_arr, norm_g, *w_packed, bd, gq, gk, gdq, gdk, cos, sin)
    return dict(zip(names, outs))


def _sbp_kernel(q_ref, k_ref, v_ref, t_ref, o_ref, acc_ref, tot_ref):
    i = pl.program_id(2)
    qb = q_ref.shape[0]
    nd = qb // LANES
    q = q_ref[...]
    tmat = t_ref[...]
    row = lax.broadcasted_iota(jnp.int32, (qb, LANES), 0)
    col = lax.broadcasted_iota(jnp.int32, (qb, LANES), 1)
    acc_ref[...] = jnp.zeros_like(acc_ref)
    tot_ref[...] = jnp.zeros_like(tot_ref)

    def tiles(js, valids):
        offs = [pl.multiple_of(j * LANES, LANES) for j in js]
        zs = [_dot_nt(q, k_ref[pl.ds(off, LANES), :]) * (D_HEAD ** -0.5) for off in offs]
        pre = [_stick_pre(z, valid, tmat) for z, valid in zip(zs, valids)]
        ws, tot = _stick_weights(pre, valids, tot_ref[...])
        tot_ref[...] = tot
        acc = acc_ref[...]
        for a, off in zip(ws, offs):
            acc = acc + _dot(_bf(a), v_ref[pl.ds(off, LANES), :])
        acc_ref[...] = acc

    diag = list(reversed(range(nd)))
    tiles([i * nd + d for d in diag], [col + d * LANES < row for d in diag])
    per = 4 if nd % 4 == 0 else 1

    def body(jj, carry):
        top = i * nd - 1 - jj * per
        tiles([top - p for p in range(per)], [None] * per)
        return carry

    lax.fori_loop(0, i * nd // per, body, 0)
    o_ref[...] = acc_ref[...]


def _sb_prompt(q_hm, k_hm, v_hm, tmat, nb, s, qb):
    h, t, _ = q_hm.shape
    nq = s // qb
    return pl.pallas_call(
        _sbp_kernel,
        grid=(nb, h, nq),
        in_specs=[pl.BlockSpec((None, qb, D_HEAD), lambda b, hh, i: (hh, b * nq + i, 0)),
                  pl.BlockSpec((None, s, D_HEAD), lambda b, hh, i: (hh, b, 0)),
                  pl.BlockSpec((None, s, D_HEAD), lambda b, hh, i: (hh, b, 0)),
                  _full_spec(tmat.shape)],
        out_specs=pl.BlockSpec((None, qb, D_HEAD), lambda b, hh, i: (hh, b * nq + i, 0)),
        out_shape=jax.ShapeDtypeStruct((h, t, D_HEAD), F32),
        scratch_shapes=[pltpu.VMEM((qb, D_HEAD), F32), pltpu.VMEM((qb, LANES), F32)],
        compiler_params=_params(("arbitrary", "arbitrary", "arbitrary")),
        name="sb_prompt",
    )(q_hm, k_hm, v_hm, tmat)


def _ret_log_gamma():
    return np.log1p(-np.exp2(-5.0 - np.arange(RET_HEADS, dtype=np.float32))).astype(np.float32)


def _ret_kernel(q_ref, k_ref, v_ref, g_ref, s0_ref, dec_ref, qd_ref, kd_ref, gn_ref, o_ref, s1_ref, s_scr, *, sdecay):
    c = pl.program_id(1)

    @pl.when(c == 0)
    def _():
        s_scr[...] = s0_ref[...]

    q = q_ref[...]
    k = k_ref[...]
    v = v_ref[...]
    outs = []
    for h in range(RET_HEADS):
        qh = _bf(q[:, h * RET_DK:(h + 1) * RET_DK])
        kh = k[:, h * RET_DK:(h + 1) * RET_DK]
        vh = _bf(v[:, h * RET_DV:(h + 1) * RET_DV])
        s = s_scr[h]
        inner = _dot_nt(qh, _bf(kh)) * dec_ref[h]
        o = _dot(_bf(inner), vh) + _dot(qh, _bf(s)) * qd_ref[h]
        s_scr[h] = sdecay[h] * s + _dot_tn(_bf(kh * kd_ref[h]), vh)
        mu = jnp.mean(o, axis=-1, keepdims=True)
        var = jnp.mean(jnp.square(o - mu), axis=-1, keepdims=True)
        outs.append((o - mu) * lax.rsqrt(var + GN_EPS))
    on = jnp.concatenate(outs, axis=1) * gn_ref[...]
    rg = g_ref[...]
    o_ref[...] = (rg * _sigmoid(rg)) * on

    @pl.when(c == pl.num_programs(1) - 1)
    def _():
        s1_ref[...] = s_scr[...]


def _retention(q, k, v, rg, s0, gn_g, c_eff):
    n, s, _ = q.shape
    cc = RET_CHUNK
    lg = _ret_log_gamma().astype(np.float64)
    idx = np.arange(cc, dtype=np.float64)
    diff = idx[:, None] - idx[None, :]
    dec = np.where(diff[None] >= 0, np.exp(np.maximum(diff, 0.0)[None] * lg[:, None, None]), 0.0)
    qd = np.exp((idx + 1.0)[None, :, None] * lg[:, None, None])
    kd = np.exp((c_eff - 1.0 - idx)[None, :, None] * lg[:, None, None])
    kd = np.where(idx[None, :, None] < c_eff, kd, 0.0)
    sdecay = tuple(float(x) for x in np.exp(c_eff * lg))

    def tok(width):
        return pl.BlockSpec((None, cc, width), lambda i, c: (i, c, 0))

    st = pl.BlockSpec((None, RET_HEADS, RET_DK, RET_DV), lambda i, c: (i, 0, 0, 0))
    return pl.pallas_call(
        functools.partial(_ret_kernel, sdecay=sdecay),
        grid=(n, s // cc),
        in_specs=[tok(256), tok(256), tok(512), tok(512), st,
                  _full_spec((RET_HEADS, cc, cc)), _full_spec((RET_HEADS, cc, 1)), _full_spec((RET_HEADS, cc, 1)),
                  _full_spec((1, RET_V_W))],
        out_specs=[tok(512), st],
        out_shape=[jax.ShapeDtypeStruct((n, s, RET_V_W), F32),
                   jax.ShapeDtypeStruct((n, RET_HEADS, RET_DK, RET_DV), F32)],
        scratch_shapes=[pltpu.VMEM((RET_HEADS, RET_DK, RET_DV), F32)],
        compiler_params=_params(("arbitrary", "arbitrary")),
        name="retention",
    )(q, k, v, rg, s0, jnp.asarray(dec, F32), jnp.asarray(qd, F32), jnp.asarray(kd, F32), gn_g)


def _dsap_kernel(iq_ref, iw_ref, ik_ref, q_ref, k_ref, v_ref, incl_ref, o_ref, *, topk):
    i = pl.program_id(1)
    s = ik_ref.shape[0]
    nq = s // Q_BLOCK

    def attend(wk):
        iq = iq_ref[...]
        iw = iw_ref[...]
        ik = ik_ref[:wk, :]
        score = jnp.zeros((Q_BLOCK, wk), F32)
        for h in range(IDX_HEADS):
            dots = _dot3_nt(iq[:, h * IDX_DIM:(h + 1) * IDX_DIM], ik) * (IDX_DIM ** -0.5)
            score = score + iw[:, h:h + 1] * jnp.maximum(dots, 0.0)
        qpos = i * Q_BLOCK + lax.broadcasted_iota(jnp.int32, (Q_BLOCK, wk), 0)
        kpos = lax.broadcasted_iota(jnp.int32, (Q_BLOCK, wk), 1)
        causal = kpos <= qpos
        score = jnp.where(causal, score + 0.0, NEG_INF)
        sel = (_topk_mask(score, topk, incl_ref[...]) > 0.5) & causal
        for g in range(DSA_KV_HEADS):
            kg = k_ref[g, :wk, :]
            vg = v_ref[g, :wk, :]
            for j in range(DSA_GROUP):
                hh = g * DSA_GROUP + j
                logits = jnp.where(sel, _dot_nt(q_ref[hh], kg) * (D_HEAD ** -0.5), NEG_INF)
                m = jnp.max(logits, axis=-1, keepdims=True)
                p = jnp.exp(logits - m)
                p = p / jnp.sum(p, axis=-1, keepdims=True)
                o_ref[hh] = _dot(_bf(p), vg)

    nbuckets = 4 if (nq % 4 == 0 and s // 4 >= topk) else 1
    per = nq // nbuckets
    for b in range(nbuckets):
        pl.when(i // per == b)(functools.partial(attend, (b + 1) * per * Q_BLOCK))


def _dsa_prompt(iq, iw, ik, q_hm, k_hm, v_hm, incl, nb, s, topk):
    t = iq.shape[0]
    nq = s // Q_BLOCK
    return pl.pallas_call(
        functools.partial(_dsap_kernel, topk=topk),
        grid=(nb, nq),
        in_specs=[pl.BlockSpec((Q_BLOCK, IDX_W), lambda b, i: (b * nq + i, 0)),
                  pl.BlockSpec((Q_BLOCK, IDX_HEADS), lambda b, i: (b * nq + i, 0)),
                  pl.BlockSpec((s, IDX_DIM), lambda b, i: (b, 0)),
                  pl.BlockSpec((DSA_HEADS, Q_BLOCK, D_HEAD), lambda b, i: (0, b * nq + i, 0)),
                  pl.BlockSpec((DSA_KV_HEADS, s, D_HEAD), lambda b, i: (0, b, 0)),
                  pl.BlockSpec((DSA_KV_HEADS, s, D_HEAD), lambda b, i: (0, b, 0)),
                  _full_spec(incl.shape)],
        out_specs=pl.BlockSpec((DSA_HEADS, Q_BLOCK, D_HEAD), lambda b, i: (0, b * nq + i, 0)),
        out_shape=jax.ShapeDtypeStruct((DSA_HEADS, t, D_HEAD), F32),
        compiler_params=_params(("arbitrary", "arbitrary")),
        name="dsa_prompt",
    )(iq, iw, ik, q_hm, k_hm, v_hm, incl)


def _merge_kernel(osb_ref, oret_ref, odsa_ref, g_ref, x_ref, gt_ref, wsb_ref, wret_ref, wdsa_ref, wout_ref, o_ref):
    d = x_ref.shape[1]
    osb = _bf(jnp.concatenate([osb_ref[h] for h in range(SB_HEADS)], axis=1))
    odsa = _bf(jnp.concatenate([odsa_ref[h] for h in range(DSA_HEADS)], axis=1))
    merged = (g_ref[:, :d] * _dot(osb, wsb_ref[...])
              + g_ref[:, d:2 * d] * _dot(_bf(oret_ref[...]), wret_ref[...])
              + g_ref[:, 2 * d:] * _dot(odsa, wdsa_ref[...]))
    o_ref[...] = x_ref[...] + gt_ref[...] * _dot(_bf(merged), wout_ref[...])


def _merge(osb_hm, oret, odsa_hm, g, x, gate, wsb, wret, wdsa, wout, tb):
    t, d = x.shape
    gt_arr, gt_spec = _mod_operand(gate, t, tb)

    def tok(width):
        return pl.BlockSpec((tb, width), lambda i: (i, 0))

    hm = pl.BlockSpec((8, tb, D_HEAD), lambda i: (0, i, 0))
    return pl.pallas_call(
        _merge_kernel,
        grid=(t // tb,),
        in_specs=[hm, tok(512), hm, tok(3 * d), tok(d), gt_spec,
                  _full_spec(wsb.shape), _full_spec(wret.shape), _full_spec(wdsa.shape), _full_spec(wout.shape)],
        out_specs=tok(d),
        out_shape=jax.ShapeDtypeStruct((t, d), F32),
        compiler_params=_params(("arbitrary",)),
        name="merge",
    )(osb_hm, oret, odsa_hm, g, x, gt_arr, wsb, wret, wdsa, wout)


def _top_vals(s, n):
    vals = []
    for _ in range(n):
        m = jnp.max(s, axis=0, keepdims=True)
        vals.append(m)
        s = jnp.where(s == m, NEG_INF, s)
    return vals


def _peer1_kernel(x_ref, sh_ref, sc_ref, ng_ref, wq_ref, keys_ref, h2t_ref, th_ref, e1_ref, s2_ref, e2_ref, q_scr):
    h = pl.program_id(1)

    @pl.when(h == 0)
    def _():
        h2 = _rms_mod(x_ref[...], ng_ref[...], sh_ref[...], sc_ref[...])
        h2t_ref[...] = _bf(h2.T)
        q = _dot(_bf(h2), wq_ref[...])
        for c in range(2 * PEER_HEADS):
            q_scr[c] = q[:, c * PEER_HALF:(c + 1) * PEER_HALF]

    s1 = _dot3_nt(keys_ref[0], q_scr[2 * h])
    s2 = _dot3_nt(keys_ref[1], q_scr[2 * h + 1])
    t1 = _top_vals(s1, PEER_TOPK)
    t2 = jnp.concatenate(_top_vals(s2, PEER_TOPK), axis=0)
    cand = jnp.concatenate([a + t2 for a in t1], axis=0)
    best = _top_vals(cand, PEER_TOPK + 1)
    z = jnp.zeros_like(best[0])
    for b in best[:PEER_TOPK]:
        z = z + jnp.exp(b - best[0])
    thr = 0.5 * (best[PEER_TOPK - 1] + best[PEER_TOPK])
    th_ref[...] = thr - s1
    e1_ref[...] = jnp.exp(s1 - t1[0]) / z
    s2_ref[...] = s2
    e2_ref[...] = jnp.exp(s2 - t2[0:1])


def _peer_scores(x, shift, scale, norm_g, wq, keys, tb):
    t, d = x.shape
    sh_arr, sh_spec = _mod_operand(shift, t, tb)
    sc_arr, sc_spec = _mod_operand(scale, t, tb)
    per_head = pl.BlockSpec((None, PEER_NKEYS, tb), lambda i, h: (h, 0, i))
    sds = jax.ShapeDtypeStruct((PEER_HEADS, PEER_NKEYS, t), F32)
    return pl.pallas_call(
        _peer1_kernel,
        grid=(t // tb, PEER_HEADS),
        in_specs=[pl.BlockSpec((tb, d), lambda i, h: (i, 0)), sh_spec, sc_spec, _full_spec((1, d)),
                  _full_spec(wq.shape),
                  pl.BlockSpec((None, 2, PEER_NKEYS, PEER_HALF), lambda i, h: (h, 0, 0, 0))],
        out_specs=[pl.BlockSpec((d, tb), lambda i, h: (0, i)), per_head, per_head, per_head, per_head],
        out_shape=[jax.ShapeDtypeStruct((d, t), BF16), sds, sds, sds, sds],
        scratch_shapes=[pltpu.VMEM((2 * PEER_HEADS, tb, PEER_HALF), F32)],
        compiler_params=_params(("arbitrary", "arbitrary")),
        name="peer_scores",
    )(x, sh_arr, sc_arr, norm_g, wq, keys)


def _peer2_kernel(h2t_ref, u_ref, vt_ref, th_ref, e1_ref, s2_ref, e2_ref, x_ref, gt_ref, o_ref,
                  acc_ref, thr_ref, e1r_ref, *act_refs):
    j = pl.program_id(1)
    tb = x_ref.shape[0]
    ch = u_ref.shape[0]

    @pl.when(j == 0)
    def _():
        acc_ref[...] = jnp.zeros_like(acc_ref)

    nr = ch // PEER_NKEYS
    wide = min(tb, 2 * LANES)
    sub = 8
    nsub = 2
    for h in range(PEER_HEADS):
        for r in range(nr):
            thr_ref[h, r] = jnp.broadcast_to(th_ref[h, r:r + 1, :], (sub, tb))
            e1r_ref[h, r] = jnp.broadcast_to(e1_ref[h, r:r + 1, :], (sub, tb))
    for cs in range(tb // wide):
        sl = slice(cs * wide, (cs + 1) * wide)
        act_ref = act_refs[cs]
        act_ref[...] = _dot(u_ref[...], h2t_ref[:, sl])
        for k0 in range(0, PEER_NKEYS, sub * nsub):
            w = [[None] * nsub for _ in range(nr)]
            for h in range(PEER_HEADS):
                s2t = [s2_ref[h, k0 + i * sub:k0 + (i + 1) * sub, sl] for i in range(nsub)]
                e2t = [e2_ref[h, k0 + i * sub:k0 + (i + 1) * sub, sl] for i in range(nsub)]
                for r in range(nr):
                    th8 = thr_ref[h, r, :, sl]
                    e18 = e1r_ref[h, r, :, sl]
                    for i in range(nsub):
                        term = jnp.where(s2t[i] >= th8, e2t[i], 0.0) * e18
                        w[r][i] = term if w[r][i] is None else w[r][i] + term
            for r in range(nr):
                for i in range(nsub):
                    rows = slice(r * PEER_NKEYS + k0 + i * sub, r * PEER_NKEYS + k0 + (i + 1) * sub)
                    a = act_ref[rows, :]
                    gelu = 0.5 * a * (1.0 + lax.erf(a * (2.0 ** -0.5)))
                    act_ref[rows, :] = gelu * w[r][i]
        acc_ref[:, sl] += _dot(vt_ref[...], _bf(act_ref[...]))

    @pl.when(j == pl.num_programs(1) - 1)
    def _():
        o_ref[...] = x_ref[...] + gt_ref[...] * acc_ref[...].T


def _peer_experts(h2t, u, vt, th, e1, s2, e2, x, gate, tb, ch):
    t, d = x.shape
    ne = u.shape[0]
    rows = ch // PEER_NKEYS
    wide = min(tb, 2 * LANES)
    gt_arr, gt_spec = _mod_operand(gate, t, tb)
    by_row = pl.BlockSpec((PEER_HEADS, rows, tb), lambda i, j: (0, j, i))
    once = pl.Buffered(1)
    by_tok = pl.BlockSpec((PEER_HEADS, PEER_NKEYS, tb), lambda i, j: (0, 0, i), pipeline_mode=once)
    return pl.pallas_call(
        _peer2_kernel,
        grid=(t // tb, ne // ch),
        in_specs=[pl.BlockSpec((d, tb), lambda i, j: (0, i), pipeline_mode=once),
                  pl.BlockSpec((ch, d), lambda i, j: (j, 0)),
                  pl.BlockSpec((d, ch), lambda i, j: (0, j)),
                  by_row, by_row, by_tok, by_tok,
                  pl.BlockSpec((tb, d), lambda i, j: (i, 0), pipeline_mode=once), gt_spec],
        out_specs=pl.BlockSpec((tb, d), lambda i, j: (i, 0)),
        out_shape=jax.ShapeDtypeStruct((t, d), F32),
        scratch_shapes=[pltpu.VMEM((d, tb), F32),
                        pltpu.VMEM((PEER_HEADS, rows, 8, tb), F32), pltpu.VMEM((PEER_HEADS, rows, 8, tb), F32)]
                       + [pltpu.VMEM((ch, wide), F32)] * (tb // wide),
        compiler_params=_params(("arbitrary", "arbitrary")),
        name="peer_experts",
    )(h2t, u, vt, th, e1, s2, e2, x, gt_arr)


def _pages_per_step(npages, want):
    g = want
    while npages % g:
        g //= 2
    return g


def _dot3(a, b):
    ah, al = _split(a)
    bh, bl = _split(b)
    return _dot(ah, bh) + _dot(ah, bl) + _dot(al, bh)


def _idx_scores(iq, iw, ikt, nt):
    sc = jnp.maximum(_dot3(iq, ikt) * (IDX_DIM ** -0.5), 0.0) * iw
    score = sc[0:nt]
    for h in range(1, IDX_HEADS):
        score = score + sc[h * nt:(h + 1) * nt]
    return score


def _sscore_kernel(pt_ref, iq_ref, iw_ref, in_ref, *refs, g):
    page_refs, (past_ref, new_ref) = refs[:g], refs[g:]
    s = pl.program_id(1)
    nt = new_ref.shape[0]
    iq = iq_ref[...]
    iw = iw_ref[...]

    @pl.when(s == 0)
    def _():
        new_ref[...] = _idx_scores(iq, iw, in_ref[...], nt)

    @pl.when(s > 0)
    def _():
        for p in range(g):
            past_ref[:, p * LANES:(p + 1) * LANES] = _idx_scores(iq, iw, page_refs[p][...], nt)


def _sample_scores(page_table, iq16, iw16, ikt_cache, ikt_new, layer, g):
    n, npages = page_table.shape
    nt = iq16.shape[1] // IDX_HEADS

    def page_spec(p):
        return pl.BlockSpec((None, None, IDX_DIM, LANES),
                            lambda i, s, pt: (layer, pt[i, (jnp.maximum(s, 1) - 1) * g + p], 0, 0))

    def per_n(rows, width):
        return pl.BlockSpec((None, rows, width), lambda i, s, pt: (i, 0, 0))

    grid_spec = pltpu.PrefetchScalarGridSpec(
        num_scalar_prefetch=1,
        grid=(n, npages // g + 1),
        in_specs=[per_n(IDX_HEADS * nt, IDX_DIM), per_n(IDX_HEADS * nt, 1), per_n(IDX_DIM, LANES)]
                 + [page_spec(p) for p in range(g)],
        out_specs=[pl.BlockSpec((None, nt, g * LANES), lambda i, s, pt: (i, 0, jnp.maximum(s, 1) - 1)),
                   per_n(nt, LANES)],
    )
    return pl.pallas_call(
        functools.partial(_sscore_kernel, g=g),
        grid_spec=grid_spec,
        out_shape=[jax.ShapeDtypeStruct((n, nt, npages * LANES), F32), jax.ShapeDtypeStruct((n, nt, LANES), F32)],
        compiler_params=_params(("arbitrary", "arbitrary")),
        name="sample_scores",
    )(page_table, iq16, iw16, ikt_new, *([ikt_cache] * g))


def _sattn_kernel(pt_ref, qsb_ref, qd_ref, t_ref, kn_ref, vn_ref, dkn_ref, dvn_ref, mn_ref, mp_ref, *refs, g):
    k_refs, v_refs, dk_refs, dv_refs = refs[:g], refs[g:2 * g], refs[2 * g:3 * g], refs[3 * g:4 * g]
    osb_ref, od_ref, acc_ref, tot_ref, mx_ref, l_ref, dacc_ref = refs[4 * g:]
    s = pl.program_id(1)
    nrow = qsb_ref.shape[0]
    drow = qd_ref.shape[0]
    qsb = qsb_ref[...]
    qd = qd_ref[...]
    tmat = t_ref[...]

    @pl.when(s == 0)
    def _():
        acc_ref[...] = jnp.zeros_like(acc_ref)
        tot_ref[...] = jnp.zeros_like(tot_ref)
        mx_ref[...] = jnp.full_like(mx_ref, -1e30)
        l_ref[...] = jnp.zeros_like(l_ref)
        dacc_ref[...] = jnp.zeros_like(dacc_ref)

    def attend(pages, sel, valid):
        zs = [_dot(qsb, _bf(kt)) * (D_HEAD ** -0.5) for kt, _, _, _ in pages]
        logits = [_dot(qd, _bf(dkt)) * (D_HEAD ** -0.5) for _, _, dkt, _ in pages]
        valids = [valid] * len(pages)
        pre = [_stick_pre(z, valid, tmat) for z in zs]
        ws, tot = _stick_weights(pre, valids, tot_ref[...])
        tot_ref[...] = tot
        acc = acc_ref[...]
        for a, (_, vt, _, _) in zip(ws, pages):
            acc = acc + _dot_nt(_bf(a), _bf(vt))
        acc_ref[...] = acc
        logits = jnp.concatenate(logits, axis=1)
        m_old = mx_ref[...]
        m_new = jnp.maximum(m_old, jnp.max(jnp.where(sel, logits, -1e30), axis=-1, keepdims=True))
        p = jnp.where(sel, jnp.exp(logits - m_new), 0.0)
        alpha = jnp.exp(m_old - m_new)
        mx_ref[...] = m_new
        l_ref[...] = alpha * l_ref[...] + jnp.sum(p, axis=-1, keepdims=True)
        dacc = alpha * dacc_ref[...]
        for c, (_, _, _, dvt) in enumerate(pages):
            dacc = dacc + _dot_nt(_bf(p[:, c * LANES:(c + 1) * LANES]), _bf(dvt))
        dacc_ref[...] = dacc

    def tile_rows(m8):
        return jnp.concatenate([m8] * (drow // m8.shape[0]), axis=0) > 0.5

    @pl.when(s == 0)
    def _():
        trow = lax.broadcasted_iota(jnp.int32, (nrow, LANES), 0) // SB_HEADS
        col = lax.broadcasted_iota(jnp.int32, (nrow, LANES), 1)
        attend([(kn_ref[...], vn_ref[...], dkn_ref[...], dvn_ref[...])], tile_rows(mn_ref[...]), col < trow)

    @pl.when(s > 0)
    def _():
        mp = mp_ref[...]
        sel = tile_rows(jnp.concatenate([mp[:, (g - 1 - p) * LANES:(g - p) * LANES] for p in range(g)], axis=1))
        attend([(k_refs[p][...], v_refs[p][...], dk_refs[p][...], dv_refs[p][...]) for p in range(g)], sel, None)

    @pl.when(s == pl.num_programs(1) - 1)
    def _():
        acc = acc_ref[...]
        hrow = lax.broadcasted_iota(jnp.int32, acc.shape, 0) % SB_HEADS
        hcol = lax.broadcasted_iota(jnp.int32, acc.shape, 1) // D_HEAD
        own = jnp.where(hrow == hcol, acc, 0.0)
        osb_ref[...] = jnp.sum(own.reshape(nrow // SB_HEADS, SB_HEADS, acc.shape[1]), axis=1)
        dacc = dacc_ref[...] / l_ref[...]
        half = drow // DSA_KV_HEADS
        grow = lax.broadcasted_iota(jnp.int32, dacc.shape, 0) // half
        gcol = lax.broadcasted_iota(jnp.int32, dacc.shape, 1) // D_HEAD
        down = jnp.where(grow == gcol, dacc, 0.0)
        od_ref[...] = down[:half] + down[half:]


def _sample_attention(page_table, qsb, qd, tmat, kt_new, vt_new, dkt_new, dvt_new, mask_new, mask_past,
                      kt_cache, vt_cache, dkt_cache, dvt_cache, layer, g):
    n, npages = page_table.shape
    nrow = qsb.shape[1]
    drow = qd.shape[1]
    nsteps = npages // g

    def page_spec(width, p):
        return pl.BlockSpec((None, None, width, LANES),
                            lambda i, s, pt: (layer, pt[i, npages - 1 - (jnp.maximum(s, 1) - 1) * g - p], 0, 0))

    def per_n(rows, width):
        return pl.BlockSpec((None, rows, width), lambda i, s, pt: (i, 0, 0))

    in_specs = ([per_n(nrow, SB_W), per_n(drow, DSA_KV_W), pl.BlockSpec(tmat.shape, lambda i, s, pt: (0, 0)),
                 per_n(SB_W, LANES), per_n(SB_W, LANES), per_n(DSA_KV_W, LANES), per_n(DSA_KV_W, LANES),
                 per_n(8, LANES),
                 pl.BlockSpec((None, 8, g * LANES), lambda i, s, pt: (i, 0, nsteps - jnp.maximum(s, 1)))]
                + [page_spec(SB_W, p) for p in range(g)] + [page_spec(SB_W, p) for p in range(g)]
                + [page_spec(DSA_KV_W, p) for p in range(g)] + [page_spec(DSA_KV_W, p) for p in range(g)])
    grid_spec = pltpu.PrefetchScalarGridSpec(
        num_scalar_prefetch=1,
        grid=(n, nsteps + 1),
        in_specs=in_specs,
        out_specs=[per_n(nrow // SB_HEADS, SB_W), per_n(drow // DSA_KV_HEADS, DSA_KV_W)],
        scratch_shapes=[pltpu.VMEM((nrow, SB_W), F32), pltpu.VMEM((nrow, LANES), F32),
                        pltpu.VMEM((drow, 1), F32), pltpu.VMEM((drow, 1), F32), pltpu.VMEM((drow, DSA_KV_W), F32)],
    )
    return pl.pallas_call(
        functools.partial(_sattn_kernel, g=g),
        grid_spec=grid_spec,
        out_shape=[jax.ShapeDtypeStruct((n, nrow // SB_HEADS, SB_W), F32),
                   jax.ShapeDtypeStruct((n, drow // DSA_KV_HEADS, DSA_KV_W), F32)],
        compiler_params=_params(("arbitrary", "arbitrary")),
        name="sample_attention",
    )(page_table, qsb, qd, tmat, kt_new, vt_new, dkt_new, dvt_new, mask_new, mask_past,
      *([kt_cache] * g), *([vt_cache] * g), *([dkt_cache] * g), *([dvt_cache] * g))


def _sthr_kernel(sc_ref, incl_ref, m_ref, *, topk, past, nt):
    rows, width = sc_ref.shape
    t = (pl.program_id(0) * rows + lax.broadcasted_iota(jnp.int32, (rows, width), 0)) % nt
    pos = lax.broadcasted_iota(jnp.int32, (rows, width), 1)
    causal = pos <= past + t
    score = jnp.where(causal, sc_ref[...] + 0.0, NEG_INF)
    m_ref[...] = jnp.where(causal, _topk_mask(score, topk, incl_ref[...]), 0.0)


def _sample_select(scores, incl, topk, past, nt):
    r, width = scores.shape
    rb = 32 if r % 32 == 0 else 8
    return pl.pallas_call(
        functools.partial(_sthr_kernel, topk=topk, past=past, nt=nt),
        grid=(r // rb,),
        in_specs=[pl.BlockSpec((rb, width), lambda i: (i, 0)), _full_spec(incl.shape)],
        out_specs=pl.BlockSpec((rb, width), lambda i: (i, 0)),
        out_shape=jax.ShapeDtypeStruct((r, width), F32),
        compiler_params=_params(("arbitrary",)),
        name="sample_select",
    )(scores, incl)


def _rope_tables(pos):
    half = D_HEAD // 2
    freqs = ROPE_BASE ** (-jnp.arange(half, dtype=F32) / half)
    ang = pos.astype(F32)[:, None] * freqs[None, :]
    cos = jnp.cos(ang)
    sin = jnp.sin(ang)
    return (jnp.tile(jnp.concatenate([cos, cos], axis=1), (1, RET_HEADS)),
            jnp.tile(jnp.concatenate([-sin, sin], axis=1), (1, RET_HEADS)))


def _tri_constants():
    j = np.arange(LANES)
    suffix = (j[:, None] > j[None, :]).astype(np.float32)
    tmat = np.concatenate([suffix, np.ones((LANES, LANES), np.float32)], axis=1)
    incl = (j[:, None] <= j[None, :]).astype(np.float32)
    blk = np.arange(SB_W) // D_HEAD
    bd = (blk[:, None] == blk[None, :]).astype(np.float32)
    return jnp.asarray(tmat, BF16), jnp.asarray(incl, BF16), jnp.asarray(bd, BF16)


def _pad_rows(a, rows):
    return jnp.pad(a, ((0, 0), (0, rows - a.shape[1]), (0, 0)))


def kernel(x_prompt, x_sample, cache_sb_k, cache_sb_v, cache_dsa_k, cache_dsa_v, cache_idx_k, state_ret, page_table, c_prompt, c_sample, norm1_g, ada1_w, ada1_b, w_in, sb_qn_g, sb_kn_g, dsa_qn_g, dsa_kn_g, ret_gn_g, w_br_sb, w_br_ret, w_br_dsa, w_out, norm2_g, ada2_w, ada2_b, peer_wq, peer_subkeys, peer_u, peer_v):
    nb, s_len, d = x_prompt.shape
    ns, t_len, _ = x_sample.shape
    depth = w_in.shape[0]
    npool, page = cache_sb_k.shape[1], cache_sb_k.shape[2]
    npages = page_table.shape[1]
    past = npages * page
    tp = nb * s_len
    ts = ns * t_len
    topk_p = min(DSA_TOPK_MAX, s_len // 4)
    topk_s = min(DSA_TOPK_MAX, (past + t_len) // 4)
    tb_p = 256
    tb_sc = 512 if tp % 512 == 0 else 256
    tb_peer = next(c for c in (1024, 512, 256) if tp % c == 0)
    peer_ch = 1024
    sb_qb = 512 if s_len % 512 == 0 else Q_BLOCK

    tmat, incl, bd = _tri_constants()
    cos_p, sin_p = _rope_tables(jnp.arange(s_len))
    cos_s, sin_s = _rope_tables(jnp.tile(past + jnp.arange(t_len), ns))

    c_all = jnp.concatenate([c_prompt, c_sample], axis=0)
    ada1 = _ada(c_all, ada1_w, ada1_b)
    ada2 = _ada(c_all, ada2_w, ada2_b)

    assert page == LANES and 8 % t_len == 0
    kt_sb = cache_sb_k.transpose(0, 1, 3, 4, 2).reshape(depth, npool, SB_W, page)
    vt_sb = cache_sb_v.transpose(0, 1, 3, 4, 2).reshape(depth, npool, SB_W, page)
    kt_dsa = cache_dsa_k.transpose(0, 1, 3, 4, 2).reshape(depth, npool, DSA_KV_W, page)
    vt_dsa = cache_dsa_v.transpose(0, 1, 3, 4, 2).reshape(depth, npool, DSA_KV_W, page)
    ikt_cache = cache_idx_k.transpose(0, 1, 3, 2)
    eye_h = jnp.eye(SB_HEADS, dtype=F32)
    eye_g = jnp.eye(DSA_KV_HEADS, dtype=F32)
    g_attn = _pages_per_step(npages, 8)
    g_score = _pages_per_step(npages, 16)

    def new_keys_t(a, width):
        a = a.reshape(ns, t_len, width).transpose(0, 2, 1)
        return jnp.pad(a, ((0, 0), (0, 0), (0, page - t_len)))

    xp = x_prompt.reshape(tp, d)
    xs = x_sample.reshape(ts, d)
    outs = {k: [] for k in ("p_sb_k", "p_sb_v", "p_dsa_k", "p_dsa_v", "p_idx_k", "p_ret",
                            "s_sb_k", "s_sb_v", "s_dsa_k", "s_dsa_v", "s_idx_k", "s_ret")}
    for l in range(depth):
        w_packed = _pack_w_in(w_in[l])
        tile8 = lambda g: jnp.tile(g, 8).reshape(1, 512)
        gains = (tile8(sb_qn_g[l]), tile8(sb_kn_g[l]), tile8(dsa_qn_g[l]), jnp.tile(dsa_kn_g[l], 2).reshape(1, 128))
        wsb, wret, wdsa, wout = _bf(w_br_sb[l]), _bf(w_br_ret[l]), _bf(w_br_dsa[l]), _bf(w_out[l])
        wq = _bf(peer_wq[l])
        u_bf = _bf(peer_u[l])
        vt_bf = _bf(peer_v[l]).T
        n1g = norm1_g[l].reshape(1, d)
        n2g = norm2_g[l].reshape(1, d)
        gn_g = ret_gn_g[l].reshape(1, RET_V_W)

        def split3(a):
            return a[:, :d], a[:, d:2 * d], a[:, 2 * d:]

        sh1p, sc1p, gt1p = split3(ada1[l, :nb])
        sh1s, sc1s, gt1s = split3(ada1[l, nb:])
        sh2p, sc2p, gt2p = split3(ada2[l, :nb])
        sh2s, sc2s, gt2s = split3(ada2[l, nb:])

        m = _inproj(xp, sh1p, sc1p, n1g, w_packed, bd, *gains, cos_p, sin_p, tb_p)
        o_sb = _sb_prompt(m["sbq_hm"], m["sbk_hm"], m["sbv_hm"], tmat, nb, s_len, sb_qb)
        o_ret, st_p = _retention(m["rq"].reshape(nb, s_len, -1), m["rk"].reshape(nb, s_len, -1),
                                 m["rv"].reshape(nb, s_len, -1), m["rg"].reshape(nb, s_len, -1),
                                 jnp.zeros((nb, RET_HEADS, RET_DK, RET_DV), F32), gn_g, RET_CHUNK)
        o_dsa = _dsa_prompt(m["iq"], m["iw"], m["ik"], m["dq_hm"], m["dk_hm"], m["dv_hm"], incl, nb, s_len, topk_p)
        xp = _merge(o_sb, o_ret.reshape(tp, -1), o_dsa, m["g"], xp, gt1p, wsb, wret, wdsa, wout, tb_p)
        h2t, th, e1, s2, e2 = _peer_scores(xp, sh2p, sc2p, n2g, wq, peer_subkeys[l], tb_sc)
        xp = _peer_experts(h2t, u_bf, vt_bf, th, e1, s2, e2, xp, gt2p, tb_peer, peer_ch)
        outs["p_sb_k"].append(m["sbk"].reshape(nb, s_len, SB_HEADS, D_HEAD))
        outs["p_sb_v"].append(m["sbv"].reshape(nb, s_len, SB_HEADS, D_HEAD))
        outs["p_dsa_k"].append(m["dk"].reshape(nb, s_len, DSA_KV_HEADS, D_HEAD))
        outs["p_dsa_v"].append(m["dv"].reshape(nb, s_len, DSA_KV_HEADS, D_HEAD))
        outs["p_idx_k"].append(m["ik"].reshape(nb, s_len, IDX_DIM))
        outs["p_ret"].append(st_p)

        m = _inproj(xs, sh1s, sc1s, n1g, w_packed, bd, *gains, cos_s, sin_s, ts)
        q_sb = m["sbq_hm"].astype(F32).reshape(SB_HEADS, ns, t_len, D_HEAD)
        qbd = jnp.einsum("hntd,hg->ntghd", q_sb, eye_h).reshape(ns, t_len * SB_HEADS, SB_W)
        iq16 = m["iq"].reshape(ns, t_len, IDX_HEADS, IDX_DIM).transpose(0, 2, 1, 3).reshape(ns, IDX_HEADS * t_len, IDX_DIM)
        iw16 = m["iw"].reshape(ns, t_len, IDX_HEADS).transpose(0, 2, 1).reshape(ns, IDX_HEADS * t_len, 1)
        sc_past, sc_new = _sample_scores(page_table, iq16, iw16, ikt_cache, new_keys_t(m["ik"], IDX_DIM), l, g_score)
        scores = jnp.concatenate([sc_past, sc_new], axis=-1)
        width = scores.shape[-1]
        mask = _sample_select(scores.reshape(ts, width), incl, topk_s, past, t_len).reshape(ns, t_len, width)
        mask8 = jnp.tile(mask, (1, 8 // t_len, 1))
        dq = m["dq_hm"].astype(F32).reshape(DSA_KV_HEADS, DSA_GROUP, ns, t_len, D_HEAD)
        qd = jnp.einsum("gjntd,gk->ngjtkd", dq, eye_g).reshape(ns, DSA_HEADS * t_len, DSA_KV_W)
        o_sb_s, o_dsa_s = _sample_attention(
            page_table, _bf(qbd), _bf(qd), tmat,
            new_keys_t(m["sbk"], SB_W), new_keys_t(m["sbv"], SB_W),
            new_keys_t(m["dk"], DSA_KV_W), new_keys_t(m["dv"], DSA_KV_W),
            mask8[:, :, past:], mask8[:, :, :past], kt_sb, vt_sb, kt_dsa, vt_dsa, l, g_attn)
        pad = lambda a: _pad_rows(a.reshape(ns, t_len, -1), RET_CHUNK)
        o_ret_s, st_s = _retention(pad(m["rq"]), pad(m["rk"]), pad(m["rv"]), pad(m["rg"]), state_ret[l], gn_g, t_len)
        o_sb_hm = o_sb_s.reshape(ts, SB_HEADS, D_HEAD).transpose(1, 0, 2)
        o_dsa_hm = o_dsa_s.reshape(ns, DSA_GROUP, t_len, DSA_KV_HEADS, D_HEAD).transpose(3, 1, 0, 2, 4)
        o_dsa_hm = o_dsa_hm.reshape(DSA_HEADS, ts, D_HEAD)
        xs = _merge(o_sb_hm, o_ret_s[:, :t_len].reshape(ts, -1), o_dsa_hm, m["g"], xs, gt1s, wsb, wret, wdsa, wout, ts)
        h2t, th, e1, s2, e2 = _peer_scores(xs, sh2s, sc2s, n2g, wq, peer_subkeys[l], ts)
        xs = _peer_experts(h2t, u_bf, vt_bf, th, e1, s2, e2, xs, gt2s, ts, peer_ch)
        outs["s_sb_k"].append(m["sbk"].reshape(ns, t_len, SB_HEADS, D_HEAD))
        outs["s_sb_v"].append(m["sbv"].reshape(ns, t_len, SB_HEADS, D_HEAD))
        outs["s_dsa_k"].append(m["dk"].reshape(ns, t_len, DSA_KV_HEADS, D_HEAD))
        outs["s_dsa_v"].append(m["dv"].reshape(ns, t_len, DSA_KV_HEADS, D_HEAD))
        outs["s_idx_k"].append(m["ik"].reshape(ns, t_len, IDX_DIM))
        outs["s_ret"].append(st_s)

    st = {k: jnp.stack(v) for k, v in outs.items()}
    return (xp.reshape(nb, s_len, d), xs.reshape(ns, t_len, d),
            st["p_sb_k"], st["p_sb_v"], st["p_dsa_k"], st["p_dsa_v"], st["p_idx_k"], st["p_ret"],
            st["s_sb_k"], st["s_sb_v"], st["s_dsa_k"], st["s_dsa_v"], st["s_idx_k"], st["s_ret"])
```
